```python
import jax, jax.numpy as jnp
from jax import lax
import numpy as np

D_MODEL = 1024
BATCH = 32
SEQ = 2048
DEPTH = 1

D_RNN = 1280
N_RNN_BLOCKS = 10
RNN_BLOCK = D_RNN // N_RNN_BLOCKS
RNN_CONV_WIDTH = 4
LRU_C = 8.0
N_HEADS = 8
HEAD_DIM = 128
D_ATTN = N_HEADS * HEAD_DIM
Q_BLOCK = 128
D_FF = 3 * D_MODEL
FFN_CONV_WIDTH = 3
RMS_EPS = 1e-6
SPLIT_SIZES = (D_RNN, D_RNN, D_ATTN, D_ATTN, D_ATTN, N_HEADS, 2 * D_MODEL)
D_IN = sum(SPLIT_SIZES)
N_MOD = 6

kernel_name = "hybrid_rglru_fox_convffn_adaln"


def rmsnorm(x, g):
    xf = x.astype(jnp.float32)
    y = xf * lax.rsqrt(jnp.mean(xf * xf, axis=-1, keepdims=True) + RMS_EPS)
    return (y * g.astype(jnp.float32)).astype(x.dtype)


def modulate(h, shift, scale):
    return h * (1 + scale[:, None, :]) + shift[:, None, :]


def causal_dwconv(x, w, b):
    K = w.shape[0]
    S = x.shape[1]
    xp = jnp.pad(x, ((0, 0), (K - 1, 0), (0, 0)))
    y = b
    for k in range(K):
        y = y + xp[:, k:k + S, :] * w[k]
    return y


def block_diag_linear(x, w, b):
    B, S, _ = x.shape
    xb = x.reshape(B, S, N_RNN_BLOCKS, RNN_BLOCK)
    return jnp.einsum('bsnc,ncd->bsnd', xb, w).reshape(B, S, D_RNN) + b


def rg_lru(x, w_a, b_a, w_i, b_i, lam):
    r = jax.nn.sigmoid(block_diag_linear(x, w_a, b_a)).astype(jnp.float32)
    i = jax.nn.sigmoid(block_diag_linear(x, w_i, b_i))
    log_a = -LRU_C * r * jax.nn.softplus(-lam.astype(jnp.float32))
    a = jnp.exp(log_a)
    mult = jnp.sqrt(-jnp.expm1(2.0 * log_a))
    u = mult * (i * x).astype(jnp.float32)

    def combine(left, right):
        a_l, b_l = left
        a_r, b_r = right
        return a_l * a_r, a_r * b_l + b_r

    _, h = lax.associative_scan(combine, (a, u), axis=1)
    return h.astype(x.dtype)


def forgetting_attention(q, k, v, log_f):
    B, S, H, Dh = q.shape
    F = jnp.cumsum(log_f, axis=1).transpose(0, 2, 1)
    qh = (q * (Dh ** -0.5)).transpose(0, 2, 1, 3)
    kh = k.transpose(0, 2, 1, 3)
    vh = v.transpose(0, 2, 1, 3)
    outs = []
    for blk in range(S // Q_BLOCK):
        q0 = blk * Q_BLOCK
        q1 = q0 + Q_BLOCK
        s = jnp.einsum('bhqd,bhkd->bhqk', qh[:, :, q0:q1], kh[:, :, :q1],
                       preferred_element_type=jnp.float32)
        s = s + F[:, :, q0:q1, None] - F[:, :, None, :q1]
        mask = (q0 + jnp.arange(Q_BLOCK))[:, None] >= jnp.arange(q1)[None, :]
        s = jnp.where(mask, s, -jnp.inf)
        p = jax.nn.softmax(s, axis=-1)
        outs.append(jnp.einsum('bhqk,bhkd->bhqd', p.astype(vh.dtype), vh[:, :, :q1]))
    o = jnp.concatenate(outs, axis=2)
    return o.transpose(0, 2, 1, 3).reshape(B, S, H * Dh)


def setup_inputs(seed: int = 0) -> dict:
    key = jax.random.key(seed)
    ks = iter(jax.random.split(key, 40))
    nrm = lambda shape, s: jax.random.normal(next(ks), shape, jnp.float32) * s
    L = DEPTH
    a0 = jax.random.uniform(next(ks), (L, D_RNN), jnp.float32, 0.9, 0.999)
    return {
        "x": nrm((BATCH, SEQ, D_MODEL), 1.0),
        "c": nrm((BATCH, D_MODEL), 1.0),
        "w_ada": nrm((L, D_MODEL, N_MOD * D_MODEL), D_MODEL ** -0.5),
        "b_ada": nrm((L, N_MOD * D_MODEL), 0.01),
        "g_norm1": 1.0 + nrm((L, D_MODEL), 0.02),
        "w_in": nrm((L, D_MODEL, D_IN), D_MODEL ** -0.5),
        "w_rnn_conv": nrm((L, RNN_CONV_WIDTH, D_RNN), RNN_CONV_WIDTH ** -0.5),
        "b_rnn_conv": nrm((L, D_RNN), 0.01),
        "w_lru_a": nrm((L, N_RNN_BLOCKS, RNN_BLOCK, RNN_BLOCK), RNN_BLOCK ** -0.5),
        "b_lru_a": nrm((L, D_RNN), 0.01),
        "w_lru_i": nrm((L, N_RNN_BLOCKS, RNN_BLOCK, RNN_BLOCK), RNN_BLOCK ** -0.5),
        "b_lru_i": nrm((L, D_RNN), 0.01),
        "lru_lambda": jnp.log(a0) - jnp.log1p(-a0),
        "b_fgate": 3.0 + nrm((L, N_HEADS), 0.1),
        "w_proj_rnn": nrm((L, D_RNN, D_MODEL), D_RNN ** -0.5),
        "w_proj_attn": nrm((L, D_ATTN, D_MODEL), D_ATTN ** -0.5),
        "w_out": nrm((L, D_MODEL, D_MODEL), D_MODEL ** -0.5),
        "g_norm2": 1.0 + nrm((L, D_MODEL), 0.02),
        "w_ffn_up": nrm((L, D_MODEL, 2 * D_FF), D_MODEL ** -0.5),
        "w_ffn_conv": nrm((L, FFN_CONV_WIDTH, D_FF), FFN_CONV_WIDTH ** -0.5),
        "b_ffn_conv": nrm((L, D_FF), 0.01),
        "w_ffn_down": nrm((L, D_FF, D_MODEL), D_FF ** -0.5),
        "w_ada_final": nrm((D_MODEL, 2 * D_MODEL), D_MODEL ** -0.5),
        "b_ada_final": nrm((2 * D_MODEL,), 0.01),
        "g_final": 1.0 + nrm((D_MODEL,), 0.02),
    }


def reference(x, c, w_ada, b_ada, g_norm1, w_in, w_rnn_conv, b_rnn_conv, w_lru_a, b_lru_a,
              w_lru_i, b_lru_i, lru_lambda, b_fgate, w_proj_rnn, w_proj_attn, w_out, g_norm2,
              w_ffn_up, w_ffn_conv, b_ffn_conv, w_ffn_down, w_ada_final, b_ada_final, g_final):
    B, S, _ = x.shape
    c_act = jax.nn.silu(c)
    split_idx = [int(v) for v in np.cumsum(SPLIT_SIZES)[:-1]]
    for l in range(DEPTH):
        mod = c_act @ w_ada[l] + b_ada[l]
        shift1, scale1, gate1, shift2, scale2, gate2 = jnp.split(mod, N_MOD, axis=-1)

        h = modulate(rmsnorm(x, g_norm1[l]), shift1, scale1)
        z = h @ w_in[l]
        xr, gr, q, k, v, fl, mg = jnp.split(z, split_idx, axis=-1)

        xr = causal_dwconv(xr, w_rnn_conv[l], b_rnn_conv[l])
        hr = rg_lru(xr, w_lru_a[l], b_lru_a[l], w_lru_i[l], b_lru_i[l], lru_lambda[l])
        y_rnn = jax.nn.gelu(gr, approximate=True) * hr

        log_f = jax.nn.log_sigmoid(fl.astype(jnp.float32) + b_fgate[l].astype(jnp.float32))
        y_attn = forgetting_attention(q.reshape(B, S, N_HEADS, HEAD_DIM),
                                      k.reshape(B, S, N_HEADS, HEAD_DIM),
                                      v.reshape(B, S, N_HEADS, HEAD_DIM), log_f)

        g_r, g_a = jnp.split(jax.nn.sigmoid(mg), 2, axis=-1)
        merged = g_r * (y_rnn @ w_proj_rnn[l]) + g_a * (y_attn @ w_proj_attn[l])
        x = x + gate1[:, None, :] * (merged @ w_out[l])

        h = modulate(rmsnorm(x, g_norm2[l]), shift2, scale2)
        gf, uf = jnp.split(h @ w_ffn_up[l], 2, axis=-1)
        gf = causal_dwconv(gf, w_ffn_conv[l], b_ffn_conv[l])
        y_ffn = (jax.nn.gelu(gf, approximate=True) * uf) @ w_ffn_down[l]
        x = x + gate2[:, None, :] * y_ffn

    shift_f, scale_f = jnp.split(c_act @ w_ada_final + b_ada_final, 2, axis=-1)
    return modulate(rmsnorm(x, g_final), shift_f, scale_f)
```

```python
import functools
import math

import jax
import jax.numpy as jnp
from jax import lax
from jax.experimental import pallas as pl
from jax.experimental.pallas import tpu as pltpu

F32 = jnp.float32
BF16 = jnp.bfloat16

N_RNN_BLOCKS = 10
RNN_BLOCK = 128
LRU_C = 8.0
N_HEADS = 8
HEAD_DIM = 128
RMS_EPS = 1e-6
N_MOD = 6

V7X_VMEM_BYTES = 64 * 1024 * 1024
VMEM_LIMIT_BYTES = V7X_VMEM_BYTES - 6 * 1024 * 1024
SUBLANES = 8

IN_TILE = 512
RNN_TILE = 512
ATTN_TILE = 256
ATTN_HEADS_PER_STEP = 2
BACK_TILE = 256
FFN_CHUNK = 512
MM_CHUNK = 512
MOD_TILE = 1024
FCUM_CHUNK = 256


def _const_spec(shape):
    nd = len(shape)
    return pl.BlockSpec(shape, lambda *_: (0,) * nd, pipeline_mode=pl.Buffered(1))


def _params(n_grid):
    return pltpu.CompilerParams(
        dimension_semantics=("arbitrary",) * n_grid, vmem_limit_bytes=VMEM_LIMIT_BYTES)


def _rmsnorm_mod(x, g, shift, scale):
    ms = jnp.mean(x * x, axis=-1, keepdims=True)
    y = x * lax.rsqrt(ms + RMS_EPS)
    return (y * g) * (1.0 + scale) + shift


def _gelu_tanh(x):
    c = math.sqrt(2.0 / math.pi)
    return x * (0.5 * (1.0 + jnp.tanh(c * (x + 0.044715 * (x * x * x)))))


def _softplus(x):
    return jnp.maximum(x, 0.0) + jnp.log1p(jnp.exp(-jnp.abs(x)))


def _chunks(n, width):
    return [(c0, min(width, n - c0)) for c0 in range(0, n, width)]


def _mod_kernel(c_ref, w_ref, b_ref, o_ref):
    c = c_ref[...]
    c_act = (c * jax.nn.sigmoid(c)).astype(BF16)
    o_ref[...] = jnp.dot(c_act, w_ref[...].astype(BF16), preferred_element_type=F32) + b_ref[...]


def _modulation(c, w, b):
    bsz, d = c.shape
    n = w.shape[1]
    tn = min(MOD_TILE, n)
    return pl.pallas_call(
        _mod_kernel,
        grid=(n // tn,),
        in_specs=[pl.BlockSpec((bsz, d), lambda j: (0, 0)),
                  pl.BlockSpec((d, tn), lambda j: (0, j)),
                  pl.BlockSpec((1, tn), lambda j: (0, j))],
        out_specs=pl.BlockSpec((bsz, tn), lambda j: (0, j)),
        out_shape=jax.ShapeDtypeStruct((bsz, n), F32),
        compiler_params=_params(1),
        name="adaln_mod",
    )(c, w, b.reshape(1, n))


def _in_kernel(x_ref, mod_ref, g1_ref, wa_ref, wqkv_ref, wflt_ref, bf_ref, wmg_ref,
               xr_ref, gr_ref, q_ref, k_ref, v_ref, lf_ref, g_ref):
    d = x_ref.shape[2]
    d_rnn = xr_ref.shape[2]
    d_attn = N_HEADS * HEAD_DIM
    h = _rmsnorm_mod(x_ref[0], g1_ref[...], mod_ref[0, :, 0:d], mod_ref[0, :, d:2 * d]).astype(BF16)

    for out_ref, base in ((xr_ref, 0), (gr_ref, d_rnn)):
        for c0, cw in _chunks(d_rnn, MM_CHUNK):
            z = jnp.dot(h, wa_ref[:, base + c0:base + c0 + cw], preferred_element_type=F32)
            out_ref[0, :, c0:c0 + cw] = z.astype(BF16)

    q_scale = HEAD_DIM ** -0.5
    for out_ref, base, scale in ((q_ref, 0, q_scale), (k_ref, d_attn, None), (v_ref, 2 * d_attn, None)):
        for c0, cw in _chunks(d_attn, MM_CHUNK):
            z = jnp.dot(h, wqkv_ref[:, base + c0:base + c0 + cw], preferred_element_type=F32)
            if scale is not None:
                z = z * scale
            for hh in range(cw // HEAD_DIM):
                out_ref[0, c0 // HEAD_DIM + hh] = z[:, hh * HEAD_DIM:(hh + 1) * HEAD_DIM].astype(BF16)

    fl_t = lax.dot_general(wflt_ref[...], h, (((1,), (1,)), ((), ())), preferred_element_type=F32)
    fl_t = fl_t[0:N_HEADS] + bf_ref[...]
    lf_ref[0] = jnp.minimum(fl_t, 0.0) - jnp.log1p(jnp.exp(-jnp.abs(fl_t)))

    for c0, cw in _chunks(2 * d, MM_CHUNK):
        z = jnp.dot(h, wmg_ref[:, c0:c0 + cw], preferred_element_type=F32)
        g_ref[0, :, c0:c0 + cw] = jax.nn.sigmoid(z).astype(BF16)


def _in_projection(x, mod, g1, w_a, w_qkv, w_flt, b_f, w_mg):
    bsz, s, d = x.shape
    d_rnn = w_a.shape[1] // 2
    tm = min(IN_TILE, s)
    tok = lambda width: pl.BlockSpec((1, tm, width), lambda b, i: (b, i, 0))
    head = pl.BlockSpec((1, N_HEADS, tm, HEAD_DIM), lambda b, i: (b, 0, i, 0))
    return pl.pallas_call(
        _in_kernel,
        grid=(bsz, s // tm),
        in_specs=[tok(d),
                  pl.BlockSpec((1, 1, N_MOD * d), lambda b, i: (b, 0, 0)),
                  _const_spec((1, d)),
                  _const_spec(w_a.shape), _const_spec(w_qkv.shape), _const_spec(w_flt.shape),
                  _const_spec((N_HEADS, 1)), _const_spec(w_mg.shape)],
        out_specs=[tok(d_rnn), tok(d_rnn), head, head, head,
                   pl.BlockSpec((1, N_HEADS, tm), lambda b, i: (b, 0, i)),
                   tok(2 * d)],
        out_shape=[jax.ShapeDtypeStruct((bsz, s, d_rnn), BF16),
                   jax.ShapeDtypeStruct((bsz, s, d_rnn), BF16),
                   jax.ShapeDtypeStruct((bsz, N_HEADS, s, HEAD_DIM), BF16),
                   jax.ShapeDtypeStruct((bsz, N_HEADS, s, HEAD_DIM), BF16),
                   jax.ShapeDtypeStruct((bsz, N_HEADS, s, HEAD_DIM), BF16),
                   jax.ShapeDtypeStruct((bsz, N_HEADS, s), F32),
                   jax.ShapeDtypeStruct((bsz, s, 2 * d), BF16)],
        compiler_params=_params(2),
        name="in_projection",
    )(x, mod.reshape(bsz, 1, N_MOD * d), g1.reshape(1, d), w_a, w_qkv, w_flt,
      b_f.reshape(N_HEADS, 1), w_mg)


def _fcum_kernel(lf_ref, o_ref):
    rows, s = lf_ref.shape
    cw = min(FCUM_CHUNK, s)
    r_i = lax.broadcasted_iota(jnp.int32, (cw, cw), 0)
    c_i = lax.broadcasted_iota(jnp.int32, (cw, cw), 1)
    tri = (r_i <= c_i).astype(BF16)
    carry = jnp.zeros((rows, 1), F32)
    for c0 in range(0, s, cw):
        v = lf_ref[:, c0:c0 + cw]
        hi = v.astype(BF16)
        r1 = v - hi.astype(F32)
        mid = r1.astype(BF16)
        lo = (r1 - mid.astype(F32)).astype(BF16)
        acc = jnp.dot(hi, tri, preferred_element_type=F32)
        acc = acc + jnp.dot(mid, tri, preferred_element_type=F32)
        acc = acc + jnp.dot(lo, tri, preferred_element_type=F32)
        out = acc + carry
        o_ref[:, c0:c0 + cw] = out
        carry = out[:, cw - 1:cw]


def _forget_cumsum(lf):
    rows, s = lf.shape
    return pl.pallas_call(
        _fcum_kernel,
        out_shape=jax.ShapeDtypeStruct((rows, s), F32),
        compiler_params=pltpu.CompilerParams(vmem_limit_bytes=VMEM_LIMIT_BYTES),
        name="forget_cumsum",
    )(lf)


def _rnn_kernel(xr_ref, gr_ref, cw_ref, cb_ref, wa_ref, ba_ref, wi_ref, bi_ref, lam_ref, y_ref,
                xe_ref, a_ref, u_ref, h_ref, p_ref, carry_ref):
    ts = xr_ref.shape[1]
    d_rnn = xr_ref.shape[2]
    kw = cw_ref.shape[0]
    seg = ts // SUBLANES

    @pl.when(pl.program_id(1) == 0)
    def _():
        xe_ref[0:SUBLANES, :] = jnp.zeros((SUBLANES, d_rnn), F32)
        carry_ref[...] = jnp.zeros_like(carry_ref)

    xe_ref[SUBLANES:SUBLANES + ts, :] = xr_ref[0].astype(F32)
    xc = cb_ref[...]
    for k in range(kw):
        xc = xc + xe_ref[pl.ds(SUBLANES - (kw - 1) + k, ts), :] * cw_ref[k:k + 1, :]
    xe_ref[0:SUBLANES, :] = xe_ref[ts:ts + SUBLANES, :]

    xcb = xc.astype(BF16)
    sp = _softplus(-lam_ref[...])
    for n in range(N_RNN_BLOCKS):
        sl = slice(n * RNN_BLOCK, (n + 1) * RNN_BLOCK)
        ra = jnp.dot(xcb[:, sl], wa_ref[n], preferred_element_type=F32) + ba_ref[:, sl]
        ia = jnp.dot(xcb[:, sl], wi_ref[n], preferred_element_type=F32) + bi_ref[:, sl]
        r = jax.nn.sigmoid(ra)
        i = jax.nn.sigmoid(ia)
        log_a = (-LRU_C * r) * sp[:, sl]
        t = jnp.tanh(log_a)
        mult = jnp.sqrt((-2.0 * t) / (1.0 - t))
        a_ref[n] = jnp.exp(log_a)
        u_ref[n] = mult * (i * xc[:, sl])

    def step(t, carry):
        idx = pl.ds(t, SUBLANES, stride=seg)
        out = []
        for n, (h, p) in enumerate(carry):
            a = a_ref[n, idx, :]
            h = a * h + u_ref[n, idx, :]
            p = a * p
            h_ref[n, idx, :] = h
            p_ref[n, idx, :] = p
            out.append((h, p))
        return tuple(out)

    ends = lax.fori_loop(
        0, seg, step,
        tuple((jnp.zeros((SUBLANES, RNN_BLOCK), F32), jnp.ones((SUBLANES, RNN_BLOCK), F32))
              for _ in range(N_RNN_BLOCKS)))

    for n, (h_end, p_end) in enumerate(ends):
        sl = slice(n * RNN_BLOCK, (n + 1) * RNN_BLOCK)
        c_in = [carry_ref[0:1, sl]]
        for j in range(1, SUBLANES):
            c_in.append(h_end[j - 1:j] + p_end[j - 1:j] * c_in[-1])
        carry_ref[0:1, sl] = h_end[SUBLANES - 1:SUBLANES] + p_end[SUBLANES - 1:SUBLANES] * c_in[-1]
        for j in range(SUBLANES):
            rows = slice(j * seg, (j + 1) * seg)
            hr = h_ref[n, rows, :] + p_ref[n, rows, :] * c_in[j]
            y_ref[0, rows, sl] = (_gelu_tanh(gr_ref[0, rows, sl].astype(F32)) * hr).astype(BF16)


def _rnn_branch(xr, gr, conv_w, conv_b, w_a, b_a, w_i, b_i, lam):
    bsz, s, d_rnn = xr.shape
    ts = min(RNN_TILE, s)
    tok = pl.BlockSpec((1, ts, d_rnn), lambda b, i: (b, i, 0))
    row = _const_spec((1, d_rnn))
    return pl.pallas_call(
        _rnn_kernel,
        grid=(bsz, s // ts),
        in_specs=[tok, tok, _const_spec(conv_w.shape), row,
                  _const_spec(w_a.shape), row, _const_spec(w_i.shape), row, row],
        out_specs=tok,
        out_shape=jax.ShapeDtypeStruct((bsz, s, d_rnn), BF16),
        scratch_shapes=[pltpu.VMEM((ts + SUBLANES, d_rnn), F32)]
        + [pltpu.VMEM((N_RNN_BLOCKS, ts, RNN_BLOCK), F32)] * 4
        + [pltpu.VMEM((SUBLANES, d_rnn), F32)],
        compiler_params=_params(2),
        name="rglru_branch",
    )(xr, gr, conv_w, conv_b.reshape(1, d_rnn), w_a, b_a.reshape(1, d_rnn), w_i,
      b_i.reshape(1, d_rnn), lam.reshape(1, d_rnn))


def _attn_kernel(q_ref, k_ref, v_ref, f_ref, o_ref):
    hb, tq = q_ref.shape[1], q_ref.shape[2]
    tk = tq
    qi = pl.program_id(2)

    for hh in range(hb):
        q = q_ref[0, hh]
        f_q = f_ref[0, 0, hh:hh + 1, pl.ds(pl.multiple_of(qi * tq, tq), tq)]
        f_t = jnp.transpose(jnp.broadcast_to(f_q, (HEAD_DIM, tq)))[:, 0:1]

        def block(j, carry, masked):
            m, l, acc = carry
            ks = pl.ds(pl.multiple_of(j * tk, tk), tk)
            s = lax.dot_general(q, k_ref[0, hh, ks, :], (((1,), (1,)), ((), ())),
                                preferred_element_type=F32)
            s = s - f_ref[0, 0, hh:hh + 1, ks]
            if masked:
                r_i = lax.broadcasted_iota(jnp.int32, (tq, tk), 0)
                c_i = lax.broadcasted_iota(jnp.int32, (tq, tk), 1)
                s = jnp.where(r_i >= c_i, s, -jnp.inf)
            m_new = jnp.maximum(m, jnp.max(s, axis=-1, keepdims=True) + f_t)
            p = jnp.exp(s - (m_new - f_t))
            alpha = jnp.exp(m - m_new)
            l = alpha * l + jnp.sum(p, axis=-1, keepdims=True)
            acc = alpha * acc + jnp.dot(p.astype(BF16), v_ref[0, hh, ks, :],
                                        preferred_element_type=F32)
            return m_new, l, acc

        init = (jnp.full((tq, 1), -jnp.inf, F32), jnp.zeros((tq, 1), F32),
                jnp.zeros((tq, HEAD_DIM), F32))
        carry = lax.fori_loop(0, qi, functools.partial(block, masked=False), init)
        _, l, acc = block(qi, carry, True)
        o_ref[0, :, hh * HEAD_DIM:(hh + 1) * HEAD_DIM] = (acc * (1.0 / l)).astype(BF16)


def _attention(q, k, v, f_cum):
    bsz, n_heads, s, dh = q.shape
    hb = ATTN_HEADS_PER_STEP
    tq = min(ATTN_TILE, s)
    f4 = f_cum.reshape(bsz, n_heads // hb, hb, s)
    kv = pl.BlockSpec((1, hb, s, dh), lambda b, g, i: (b, g, 0, 0))
    return pl.pallas_call(
        _attn_kernel,
        grid=(bsz, n_heads // hb, s // tq),
        in_specs=[pl.BlockSpec((1, hb, tq, dh), lambda b, g, i: (b, g, i, 0)), kv, kv,
                  pl.BlockSpec((1, 1, hb, s), lambda b, g, i: (b, g, 0, 0))],
        out_specs=pl.BlockSpec((1, tq, hb * dh), lambda b, g, i: (b, i, g)),
        out_shape=jax.ShapeDtypeStruct((bsz, s, n_heads * dh), BF16),
        compiler_params=_params(3),
        name="forgetting_attention",
    )(q, k, v, f4)


def _back_kernel(x_ref, yr_ref, ya_ref, g_ref, mod_ref, g2_ref, wpr_ref, wpa_ref, wo_ref,
                 wup_ref, cw_ref, cb_ref, wdn_ref, modf_ref, gf_ref, o_ref, ge_ref, *, final):
    tm, d = x_ref.shape[1], x_ref.shape[2]
    d_ff = wdn_ref.shape[0]
    kw = cw_ref.shape[0]
    mod = lambda idx: mod_ref[0, :, idx * d:(idx + 1) * d]

    @pl.when(pl.program_id(1) == 0)
    def _():
        ge_ref[0:SUBLANES, :] = jnp.zeros((SUBLANES, d_ff), F32)

    pr = jnp.dot(yr_ref[0], wpr_ref[...], preferred_element_type=F32)
    pa = jnp.dot(ya_ref[0], wpa_ref[...], preferred_element_type=F32)
    merged = g_ref[0, :, 0:d].astype(F32) * pr + g_ref[0, :, d:2 * d].astype(F32) * pa
    x1 = x_ref[0] + mod(2) * jnp.dot(merged.astype(BF16), wo_ref[...], preferred_element_type=F32)

    h2 = _rmsnorm_mod(x1, g2_ref[...], mod(3), mod(4)).astype(BF16)
    y = jnp.zeros((tm, d), F32)
    for c0, cw in _chunks(d_ff, FFN_CHUNK):
        cs = slice(c0, c0 + cw)
        gfc = jnp.dot(h2, wup_ref[:, cs], preferred_element_type=F32)
        ufc = jnp.dot(h2, wup_ref[:, d_ff + c0:d_ff + c0 + cw], preferred_element_type=F32)
        ge_ref[SUBLANES:SUBLANES + tm, cs] = gfc
        conv = cb_ref[:, cs]
        for k in range(kw - 1):
            conv = conv + ge_ref[pl.ds(SUBLANES - (kw - 1) + k, tm), cs] * cw_ref[k:k + 1, cs]
        conv = conv + gfc * cw_ref[kw - 1:kw, cs]
        act = (_gelu_tanh(conv) * ufc).astype(BF16)
        y = y + jnp.dot(act, wdn_ref[cs, :], preferred_element_type=F32)
    ge_ref[0:SUBLANES, :] = ge_ref[tm:tm + SUBLANES, :]

    x2 = x1 + mod(5) * y
    if final:
        x2 = _rmsnorm_mod(x2, gf_ref[...], modf_ref[0, :, 0:d], modf_ref[0, :, d:2 * d])
    o_ref[0] = x2


def _back_half(x, y_rnn, y_attn, g, mod, g2, w_pr, w_pa, w_o, w_up, conv_w, conv_b, w_dn,
               mod_f, g_f, final):
    bsz, s, d = x.shape
    d_ff = w_dn.shape[0]
    tm = min(BACK_TILE, s)
    tok = lambda width: pl.BlockSpec((1, tm, width), lambda b, i: (b, i, 0))
    return pl.pallas_call(
        functools.partial(_back_kernel, final=final),
        grid=(bsz, s // tm),
        in_specs=[tok(d), tok(y_rnn.shape[2]), tok(y_attn.shape[2]), tok(2 * d),
                  pl.BlockSpec((1, 1, N_MOD * d), lambda b, i: (b, 0, 0)),
                  _const_spec((1, d)),
                  _const_spec(w_pr.shape), _const_spec(w_pa.shape), _const_spec(w_o.shape),
                  _const_spec(w_up.shape), _const_spec(conv_w.shape), _const_spec((1, d_ff)),
                  _const_spec(w_dn.shape),
                  pl.BlockSpec((1, 1, 2 * d), lambda b, i: (b, 0, 0)),
                  _const_spec((1, d))],
        out_specs=tok(d),
        out_shape=jax.ShapeDtypeStruct((bsz, s, d), F32),
        scratch_shapes=[pltpu.VMEM((tm + SUBLANES, d_ff), F32)],
        compiler_params=_params(2),
        name="merge_ffn",
    )(x, y_rnn, y_attn, g, mod.reshape(bsz, 1, N_MOD * d), g2.reshape(1, d), w_pr, w_pa, w_o,
      w_up, conv_w, conv_b.reshape(1, d_ff), w_dn, mod_f.reshape(bsz, 1, 2 * d), g_f.reshape(1, d))


def kernel(x, c, w_ada, b_ada, g_norm1, w_in, w_rnn_conv, b_rnn_conv, w_lru_a, b_lru_a, w_lru_i, b_lru_i, lru_lambda, b_fgate, w_proj_rnn, w_proj_attn, w_out, g_norm2, w_ffn_up, w_ffn_conv, b_ffn_conv, w_ffn_down, w_ada_final, b_ada_final, g_final):
    bsz, s, d = x.shape
    depth = w_ada.shape[0]
    d_rnn = w_rnn_conv.shape[2]
    d_attn = N_HEADS * HEAD_DIM
    mod_f = _modulation(c, w_ada_final, b_ada_final)
    for l in range(depth):
        mod = _modulation(c, w_ada[l], b_ada[l])
        o_qkv = 2 * d_rnn
        o_fl = o_qkv + 3 * d_attn
        o_mg = o_fl + N_HEADS
        w_a = w_in[l][:, :o_qkv].astype(BF16)
        w_qkv = w_in[l][:, o_qkv:o_fl].astype(BF16)
        w_flt = jnp.pad(w_in[l][:, o_fl:o_mg].T, ((0, 2 * SUBLANES - N_HEADS), (0, 0))).astype(BF16)
        w_mg = w_in[l][:, o_mg:].astype(BF16)

        xr, gr, q, k, v, lf, g = _in_projection(x, mod, g_norm1[l], w_a, w_qkv, w_flt, b_fgate[l], w_mg)
        f_cum = _forget_cumsum(lf.reshape(bsz * N_HEADS, s)).reshape(bsz, N_HEADS, s)
        y_rnn = _rnn_branch(xr, gr, w_rnn_conv[l], b_rnn_conv[l], w_lru_a[l].astype(BF16), b_lru_a[l],
                            w_lru_i[l].astype(BF16), b_lru_i[l], lru_lambda[l])
        y_attn = _attention(q, k, v, f_cum)
        x = _back_half(x, y_rnn, y_attn, g, mod, g_norm2[l], w_proj_rnn[l].astype(BF16),
                       w_proj_attn[l].astype(BF16), w_out[l].astype(BF16), w_ffn_up[l].astype(BF16),
                       w_ffn_conv[l], b_ffn_conv[l], w_ffn_down[l].astype(BF16), mod_f, g_final,
                       final=(l == depth - 1))
    return x
```

```python
import functools
import math

import jax
import jax.numpy as jnp
from jax import lax
from jax.experimental import pallas as pl
from jax.experimental.pallas import tpu as pltpu

F32 = jnp.float32
BF16 = jnp.bfloat16

N_RNN_BLOCKS = 10
RNN_BLOCK = 128
LRU_C = 8.0
N_HEADS = 8
HEAD_DIM = 128
RMS_EPS = 1e-6
N_MOD = 6

V7X_VMEM_BYTES = 64 * 1024 * 1024
VMEM_LIMIT_BYTES = V7X_VMEM_BYTES - 6 * 1024 * 1024
SUBLANES = 8

IN_TILE = 512
RNN_TILE = 512
ATTN_TILE = 256
ATTN_HEADS_PER_STEP = 2
BACK_TILE = 512
FFN_CHUNK = 512
MM_CHUNK = 512
MOD_TILE = 1024
FCUM_CHUNK = 256


def _const_spec(shape):
    nd = len(shape)
    return pl.BlockSpec(shape, lambda *_: (0,) * nd, pipeline_mode=pl.Buffered(1))


def _params(n_grid):
    return pltpu.CompilerParams(
        dimension_semantics=("arbitrary",) * n_grid, vmem_limit_bytes=VMEM_LIMIT_BYTES)


def _rmsnorm_mod(x, g, shift, scale):
    ms = jnp.mean(x * x, axis=-1, keepdims=True)
    y = x * lax.rsqrt(ms + RMS_EPS)
    return (y * g) * (1.0 + scale) + shift


def _gelu_tanh(x):
    c = math.sqrt(2.0 / math.pi)
    return x * (0.5 + 0.5 * jnp.tanh(x * (c + (0.044715 * c) * (x * x))))


def _softplus(x):
    return jnp.maximum(x, 0.0) + jnp.log1p(jnp.exp(-jnp.abs(x)))


def _chunks(n, width):
    return [(c0, min(width, n - c0)) for c0 in range(0, n, width)]


def _mod_kernel(c_ref, w_ref, b_ref, o_ref):
    c = c_ref[...]
    c_act = (c * jax.nn.sigmoid(c)).astype(BF16)
    o_ref[...] = jnp.dot(c_act, w_ref[...].astype(BF16), preferred_element_type=F32) + b_ref[...]


def _modulation(c, w, b):
    bsz, d = c.shape
    n = w.shape[1]
    tn = min(MOD_TILE, n)
    return pl.pallas_call(
        _mod_kernel,
        grid=(n // tn,),
        in_specs=[pl.BlockSpec((bsz, d), lambda j: (0, 0)),
                  pl.BlockSpec((d, tn), lambda j: (0, j)),
                  pl.BlockSpec((1, tn), lambda j: (0, j))],
        out_specs=pl.BlockSpec((bsz, tn), lambda j: (0, j)),
        out_shape=jax.ShapeDtypeStruct((bsz, n), F32),
        compiler_params=_params(1),
        name="adaln_mod",
    )(c, w, b.reshape(1, n))


def _in_kernel(x_ref, mod_ref, g1_ref, wa_ref, wqkv_ref, wflt_ref, bf_ref, wmg_ref,
               xr_ref, gr_ref, q_ref, k_ref, v_ref, lf_ref, g_ref):
    d = x_ref.shape[2]
    d_rnn = xr_ref.shape[2]
    d_attn = N_HEADS * HEAD_DIM
    h = _rmsnorm_mod(x_ref[0], g1_ref[...], mod_ref[0, :, 0:d], mod_ref[0, :, d:2 * d]).astype(BF16)

    for out_ref, base in ((xr_ref, 0), (gr_ref, d_rnn)):
        for c0, cw in _chunks(d_rnn, MM_CHUNK):
            z = jnp.dot(h, wa_ref[:, base + c0:base + c0 + cw], preferred_element_type=F32)
            out_ref[0, :, c0:c0 + cw] = z.astype(BF16)

    q_scale = HEAD_DIM ** -0.5
    for out_ref, base, scale in ((q_ref, 0, q_scale), (k_ref, d_attn, None), (v_ref, 2 * d_attn, None)):
        for c0, cw in _chunks(d_attn, MM_CHUNK):
            z = jnp.dot(h, wqkv_ref[:, base + c0:base + c0 + cw], preferred_element_type=F32)
            if scale is not None:
                z = z * scale
            for hh in range(cw // HEAD_DIM):
                out_ref[0, c0 // HEAD_DIM + hh] = z[:, hh * HEAD_DIM:(hh + 1) * HEAD_DIM].astype(BF16)

    fl_t = lax.dot_general(wflt_ref[...], h, (((1,), (1,)), ((), ())), preferred_element_type=F32)
    fl_t = fl_t[0:N_HEADS] + bf_ref[...]
    lf_ref[0] = jnp.minimum(fl_t, 0.0) - jnp.log1p(jnp.exp(-jnp.abs(fl_t)))

    for c0, cw in _chunks(2 * d, MM_CHUNK):
        z = jnp.dot(h, wmg_ref[:, c0:c0 + cw], preferred_element_type=F32)
        g_ref[0, :, c0:c0 + cw] = jax.nn.sigmoid(z).astype(BF16)


def _in_projection(x, mod, g1, w_a, w_qkv, w_flt, b_f, w_mg):
    bsz, s, d = x.shape
    d_rnn = w_a.shape[1] // 2
    tm = min(IN_TILE, s)
    tok = lambda width: pl.BlockSpec((1, tm, width), lambda b, i: (b, i, 0))
    head = pl.BlockSpec((1, N_HEADS, tm, HEAD_DIM), lambda b, i: (b, 0, i, 0))
    return pl.pallas_call(
        _in_kernel,
        grid=(bsz, s // tm),
        in_specs=[tok(d),
                  pl.BlockSpec((1, 1, N_MOD * d), lambda b, i: (b, 0, 0)),
                  _const_spec((1, d)),
                  _const_spec(w_a.shape), _const_spec(w_qkv.shape), _const_spec(w_flt.shape),
                  _const_spec((N_HEADS, 1)), _const_spec(w_mg.shape)],
        out_specs=[tok(d_rnn), tok(d_rnn), head, head, head,
                   pl.BlockSpec((1, N_HEADS, tm), lambda b, i: (b, 0, i)),
                   tok(2 * d)],
        out_shape=[jax.ShapeDtypeStruct((bsz, s, d_rnn), BF16),
                   jax.ShapeDtypeStruct((bsz, s, d_rnn), BF16),
                   jax.ShapeDtypeStruct((bsz, N_HEADS, s, HEAD_DIM), BF16),
                   jax.ShapeDtypeStruct((bsz, N_HEADS, s, HEAD_DIM), BF16),
                   jax.ShapeDtypeStruct((bsz, N_HEADS, s, HEAD_DIM), BF16),
                   jax.ShapeDtypeStruct((bsz, N_HEADS, s), F32),
                   jax.ShapeDtypeStruct((bsz, s, 2 * d), BF16)],
        compiler_params=_params(2),
        name="in_projection",
    )(x, mod.reshape(bsz, 1, N_MOD * d), g1.reshape(1, d), w_a, w_qkv, w_flt,
      b_f.reshape(N_HEADS, 1), w_mg)


def _fcum_kernel(lf_ref, o_ref):
    rows, s = lf_ref.shape
    cw = min(FCUM_CHUNK, s)
    r_i = lax.broadcasted_iota(jnp.int32, (cw, cw), 0)
    c_i = lax.broadcasted_iota(jnp.int32, (cw, cw), 1)
    tri = (r_i <= c_i).astype(BF16)
    carry = jnp.zeros((rows, 1), F32)
    for c0 in range(0, s, cw):
        v = lf_ref[:, c0:c0 + cw]
        hi = v.astype(BF16)
        r1 = v - hi.astype(F32)
        mid = r1.astype(BF16)
        lo = (r1 - mid.astype(F32)).astype(BF16)
        acc = jnp.dot(hi, tri, preferred_element_type=F32)
        acc = acc + jnp.dot(mid, tri, preferred_element_type=F32)
        acc = acc + jnp.dot(lo, tri, preferred_element_type=F32)
        out = acc + carry
        o_ref[:, c0:c0 + cw] = out
        carry = out[:, cw - 1:cw]


def _forget_cumsum(lf):
    rows, s = lf.shape
    return pl.pallas_call(
        _fcum_kernel,
        out_shape=jax.ShapeDtypeStruct((rows, s), F32),
        compiler_params=pltpu.CompilerParams(vmem_limit_bytes=VMEM_LIMIT_BYTES),
        name="forget_cumsum",
    )(lf)


def _rnn_kernel(xr_ref, gr_ref, cw_ref, cb_ref, wa_ref, ba_ref, wi_ref, bi_ref, lam_ref, y_ref,
                xe_ref, a_ref, u_ref, carry_ref):
    ts = xr_ref.shape[1]
    d_rnn = xr_ref.shape[2]
    kw = cw_ref.shape[0]
    seg = ts // SUBLANES

    @pl.when(pl.program_id(1) == 0)
    def _():
        xe_ref[0:SUBLANES, :] = jnp.zeros((SUBLANES, d_rnn), F32)
        carry_ref[...] = jnp.zeros_like(carry_ref)

    xe_ref[SUBLANES:SUBLANES + ts, :] = xr_ref[0].astype(F32)
    xc = cb_ref[...]
    for k in range(kw):
        xc = xc + xe_ref[pl.ds(SUBLANES - (kw - 1) + k, ts), :] * cw_ref[k:k + 1, :]
    xe_ref[0:SUBLANES, :] = xe_ref[ts:ts + SUBLANES, :]

    xcb = xc.astype(BF16)
    sp_scaled = -LRU_C * _softplus(-lam_ref[...])
    for n in range(N_RNN_BLOCKS):
        sl = slice(n * RNN_BLOCK, (n + 1) * RNN_BLOCK)
        ra = jnp.dot(xcb[:, sl], wa_ref[n], preferred_element_type=F32) + ba_ref[:, sl]
        ia = jnp.dot(xcb[:, sl], wi_ref[n], preferred_element_type=F32) + bi_ref[:, sl]
        r = jax.nn.sigmoid(ra)
        i = jax.nn.sigmoid(ia)
        log_a = r * sp_scaled[:, sl]
        t = jnp.tanh(log_a)
        w = -2.0 * t
        mult = jnp.where(w > 0.0, w * lax.rsqrt(w * (1.0 - t)), 0.0)
        a_ref[n, 0:ts, :] = jnp.exp(log_a)
        u_ref[n, 0:ts, :] = mult * (i * xc[:, sl])
        a_ref[n, ts:ts + SUBLANES, :] = jnp.ones((SUBLANES, RNN_BLOCK), F32)
        u_ref[n, ts:ts + SUBLANES, :] = jnp.zeros((SUBLANES, RNN_BLOCK), F32)

    pitch = seg + 1
    n_unroll = 5 if pitch % 5 == 0 else 1

    def summarize(t, carry):
        idx = pl.ds(t, SUBLANES, stride=pitch)
        out = []
        for n, (h, p) in enumerate(carry):
            a = a_ref[n, idx, :]
            out.append((a * h + u_ref[n, idx, :], a * p))
        return tuple(out)

    ends = lax.fori_loop(
        0, pitch, summarize,
        tuple((jnp.zeros((SUBLANES, RNN_BLOCK), F32), jnp.ones((SUBLANES, RNN_BLOCK), F32))
              for _ in range(N_RNN_BLOCKS)), unroll=n_unroll)

    sub = lax.broadcasted_iota(jnp.int32, (SUBLANES, RNN_BLOCK), 0)
    starts = []
    for n, (h_end, p_end) in enumerate(ends):
        sl = slice(n * RNN_BLOCK, (n + 1) * RNN_BLOCK)
        c = carry_ref[0:1, sl]
        start = jnp.broadcast_to(c, (SUBLANES, RNN_BLOCK))
        for j in range(1, SUBLANES):
            c = h_end[j - 1:j] + p_end[j - 1:j] * c
            start = jnp.where(sub == j, c, start)
        carry_ref[0:1, sl] = h_end[SUBLANES - 1:SUBLANES] + p_end[SUBLANES - 1:SUBLANES] * c
        starts.append(start)

    def rescan(t, hs):
        idx = pl.ds(t, SUBLANES, stride=pitch)
        out = []
        for n, h in enumerate(hs):
            h = a_ref[n, idx, :] * h + u_ref[n, idx, :]
            u_ref[n, idx, :] = h
            out.append(h)
        return tuple(out)

    lax.fori_loop(0, pitch, rescan, tuple(starts), unroll=n_unroll)

    for n in range(N_RNN_BLOCKS):
        sl = slice(n * RNN_BLOCK, (n + 1) * RNN_BLOCK)
        y_ref[0, :, sl] = (_gelu_tanh(gr_ref[0, :, sl].astype(F32)) * u_ref[n, 0:ts, :]).astype(BF16)


def _rnn_branch(xr, gr, conv_w, conv_b, w_a, b_a, w_i, b_i, lam):
    bsz, s, d_rnn = xr.shape
    ts = min(RNN_TILE, s)
    tok = pl.BlockSpec((1, ts, d_rnn), lambda b, i: (b, i, 0))
    row = _const_spec((1, d_rnn))
    return pl.pallas_call(
        _rnn_kernel,
        grid=(bsz, s // ts),
        in_specs=[tok, tok, _const_spec(conv_w.shape), row,
                  _const_spec(w_a.shape), row, _const_spec(w_i.shape), row, row],
        out_specs=tok,
        out_shape=jax.ShapeDtypeStruct((bsz, s, d_rnn), BF16),
        scratch_shapes=[pltpu.VMEM((ts + SUBLANES, d_rnn), F32)]
        + [pltpu.VMEM((N_RNN_BLOCKS, ts + SUBLANES, RNN_BLOCK), F32)] * 2
        + [pltpu.VMEM((SUBLANES, d_rnn), F32)],
        compiler_params=_params(2),
        name="rglru_branch",
    )(xr, gr, conv_w, conv_b.reshape(1, d_rnn), w_a, b_a.reshape(1, d_rnn), w_i,
      b_i.reshape(1, d_rnn), lam.reshape(1, d_rnn))


def _attn_kernel(q_ref, k_ref, v_ref, f_ref, o_ref):
    hb, s_len = q_ref.shape[1], q_ref.shape[2]
    tq = min(ATTN_TILE, s_len)
    nt = (((1,), (1,)), ((), ()))
    causal = (lax.broadcasted_iota(jnp.int32, (tq, tq), 0)
              >= lax.broadcasted_iota(jnp.int32, (tq, tq), 1))

    for hh in range(hb):
        for i in range(s_len // tq):
            lo, hi = i * tq, (i + 1) * tq
            q = q_ref[0, hh, lo:hi, :]
            f_q = f_ref[0, 0, hh:hh + 1, lo:hi]
            f_t = jnp.transpose(jnp.broadcast_to(f_q, (HEAD_DIM, tq)))[:, 0:1]

            s_d = lax.dot_general(q, k_ref[0, hh, lo:hi, :], nt, preferred_element_type=F32) - f_q
            s_d = jnp.where(causal, s_d, -jnp.inf)
            m = jnp.max(s_d, axis=-1, keepdims=True)
            if i > 0:
                s_o = (lax.dot_general(q, k_ref[0, hh, 0:lo, :], nt, preferred_element_type=F32)
                       - f_ref[0, 0, hh:hh + 1, 0:lo])
                m = jnp.maximum(m, jnp.max(s_o, axis=-1, keepdims=True))
            shift = (m + f_t) - f_t
            p_d = jnp.exp(s_d - shift)
            l = jnp.sum(p_d, axis=-1, keepdims=True)
            acc = jnp.dot(p_d.astype(BF16), v_ref[0, hh, lo:hi, :], preferred_element_type=F32)
            if i > 0:
                p_o = jnp.exp(s_o - shift)
                l = l + jnp.sum(p_o, axis=-1, keepdims=True)
                acc = acc + jnp.dot(p_o.astype(BF16), v_ref[0, hh, 0:lo, :],
                                    preferred_element_type=F32)
            o_ref[0, lo:hi, hh * HEAD_DIM:(hh + 1) * HEAD_DIM] = (acc * (1.0 / l)).astype(BF16)


def _attention(q, k, v, f_cum):
    bsz, n_heads, s, dh = q.shape
    hb = ATTN_HEADS_PER_STEP
    f4 = f_cum.reshape(bsz, n_heads // hb, hb, s)
    qkv = pl.BlockSpec((1, hb, s, dh), lambda b, g: (b, g, 0, 0))
    return pl.pallas_call(
        _attn_kernel,
        grid=(bsz, n_heads // hb),
        in_specs=[qkv, qkv, qkv, pl.BlockSpec((1, 1, hb, s), lambda b, g: (b, g, 0, 0))],
        out_specs=pl.BlockSpec((1, s, hb * dh), lambda b, g: (b, 0, g)),
        out_shape=jax.ShapeDtypeStruct((bsz, s, n_heads * dh), BF16),
        compiler_params=_params(2),
        name="forgetting_attention",
    )(q, k, v, f4)


def _back_kernel(x_ref, yr_ref, ya_ref, g_ref, mod_ref, g2_ref, wpr_ref, wpa_ref, wo_ref,
                 wup_ref, cw_ref, cb_ref, wdn_ref, modf_ref, gf_ref, o_ref, ge_ref, halo_ref, *, final):
    tm, d = x_ref.shape[1], x_ref.shape[2]
    d_ff = wdn_ref.shape[0]
    kw = cw_ref.shape[0]
    mod = lambda idx: mod_ref[0, :, idx * d:(idx + 1) * d]

    @pl.when(pl.program_id(1) == 0)
    def _():
        halo_ref[...] = jnp.zeros_like(halo_ref)

    pr = jnp.dot(yr_ref[0], wpr_ref[...], preferred_element_type=F32)
    pa = jnp.dot(ya_ref[0], wpa_ref[...], preferred_element_type=F32)
    merged = g_ref[0, :, 0:d].astype(F32) * pr + g_ref[0, :, d:2 * d].astype(F32) * pa
    x1 = x_ref[0] + mod(2) * jnp.dot(merged.astype(BF16), wo_ref[...], preferred_element_type=F32)

    h2 = _rmsnorm_mod(x1, g2_ref[...], mod(3), mod(4)).astype(BF16)
    y = jnp.zeros((tm, d), F32)
    for c0, cw in _chunks(d_ff, FFN_CHUNK):
        cs = slice(c0, c0 + cw)
        gfc = jnp.dot(h2, wup_ref[:, cs], preferred_element_type=F32)
        ufc = jnp.dot(h2, wup_ref[:, d_ff + c0:d_ff + c0 + cw], preferred_element_type=F32)
        ge_ref[0:SUBLANES, 0:cw] = halo_ref[:, cs]
        ge_ref[SUBLANES:SUBLANES + tm, 0:cw] = gfc
        conv = cb_ref[:, cs]
        for k in range(kw - 1):
            conv = conv + ge_ref[pl.ds(SUBLANES - (kw - 1) + k, tm), 0:cw] * cw_ref[k:k + 1, cs]
        conv = conv + gfc * cw_ref[kw - 1:kw, cs]
        halo_ref[:, cs] = ge_ref[tm:tm + SUBLANES, 0:cw]
        act = (_gelu_tanh(conv) * ufc).astype(BF16)
        y = y + jnp.dot(act, wdn_ref[cs, :], preferred_element_type=F32)

    x2 = x1 + mod(5) * y
    if final:
        x2 = _rmsnorm_mod(x2, gf_ref[...], modf_ref[0, :, 0:d], modf_ref[0, :, d:2 * d])
    o_ref[0] = x2


def _back_half(x, y_rnn, y_attn, g, mod, g2, w_pr, w_pa, w_o, w_up, conv_w, conv_b, w_dn,
               mod_f, g_f, final):
    bsz, s, d = x.shape
    d_ff = w_dn.shape[0]
    tm = min(BACK_TILE, s)
    tok = lambda width: pl.BlockSpec((1, tm, width), lambda b, i: (b, i, 0))
    return pl.pallas_call(
        functools.partial(_back_kernel, final=final),
        grid=(bsz, s // tm),
        in_specs=[tok(d), tok(y_rnn.shape[2]), tok(y_attn.shape[2]), tok(2 * d),
                  pl.BlockSpec((1, 1, N_MOD * d), lambda b, i: (b, 0, 0)),
                  _const_spec((1, d)),
                  _const_spec(w_pr.shape), _const_spec(w_pa.shape), _const_spec(w_o.shape),
                  _const_spec(w_up.shape), _const_spec(conv_w.shape), _const_spec((1, d_ff)),
                  _const_spec(w_dn.shape),
                  pl.BlockSpec((1, 1, 2 * d), lambda b, i: (b, 0, 0)),
                  _const_spec((1, d))],
        out_specs=tok(d),
        out_shape=jax.ShapeDtypeStruct((bsz, s, d), F32),
        scratch_shapes=[pltpu.VMEM((tm + SUBLANES, FFN_CHUNK), F32),
                        pltpu.VMEM((SUBLANES, d_ff), F32)],
        compiler_params=_params(2),
        name="merge_ffn",
    )(x, y_rnn, y_attn, g, mod.reshape(bsz, 1, N_MOD * d), g2.reshape(1, d), w_pr, w_pa, w_o,
      w_up, conv_w, conv_b.reshape(1, d_ff), w_dn, mod_f.reshape(bsz, 1, 2 * d), g_f.reshape(1, d))


def kernel(x, c, w_ada, b_ada, g_norm1, w_in, w_rnn_conv, b_rnn_conv, w_lru_a, b_lru_a, w_lru_i, b_lru_i, lru_lambda, b_fgate, w_proj_rnn, w_proj_attn, w_out, g_norm2, w_ffn_up, w_ffn_conv, b_ffn_conv, w_ffn_down, w_ada_final, b_ada_final, g_final):
    bsz, s, d = x.shape
    depth = w_ada.shape[0]
    d_rnn = w_rnn_conv.shape[2]
    d_attn = N_HEADS * HEAD_DIM
    mod_f = _modulation(c, w_ada_final, b_ada_final)
    for l in range(depth):
        mod = _modulation(c, w_ada[l], b_ada[l])
        o_qkv = 2 * d_rnn
        o_fl = o_qkv + 3 * d_attn
        o_mg = o_fl + N_HEADS
        w_a = w_in[l][:, :o_qkv].astype(BF16)
        w_qkv = w_in[l][:, o_qkv:o_fl].astype(BF16)
        w_flt = jnp.pad(w_in[l][:, o_fl:o_mg].T, ((0, 2 * SUBLANES - N_HEADS), (0, 0))).astype(BF16)
        w_mg = w_in[l][:, o_mg:].astype(BF16)

        xr, gr, q, k, v, lf, g = _in_projection(x, mod, g_norm1[l], w_a, w_qkv, w_flt, b_fgate[l], w_mg)
        f_cum = _forget_cumsum(lf.reshape(bsz * N_HEADS, s)).reshape(bsz, N_HEADS, s)
        y_rnn = _rnn_branch(xr, gr, w_rnn_conv[l], b_rnn_conv[l], w_lru_a[l].astype(BF16), b_lru_a[l],
                            w_lru_i[l].astype(BF16), b_lru_i[l], lru_lambda[l])
        y_attn = _attention(q, k, v, f_cum)
        x = _back_half(x, y_rnn, y_attn, g, mod, g_norm2[l], w_proj_rnn[l].astype(BF16),
                       w_proj_attn[l].astype(BF16), w_out[l].astype(BF16), w_ffn_up[l].astype(BF16),
                       w_ffn_conv[l], b_ffn_conv[l], w_ffn_down[l].astype(BF16), mod_f, g_final,
                       final=(l == depth - 1))
    return x
```

```python
import functools
import math

import jax
import jax.numpy as jnp
from jax import lax
from jax.experimental import pallas as pl
from jax.experimental.pallas import tpu as pltpu

F32 = jnp.float32
BF16 = jnp.bfloat16

N_RNN_BLOCKS = 10
RNN_BLOCK = 128
LRU_C = 8.0
N_HEADS = 8
HEAD_DIM = 128
RMS_EPS = 1e-6
N_MOD = 6

V7X_VMEM_BYTES = 64 * 1024 * 1024
VMEM_LIMIT_BYTES = V7X_VMEM_BYTES - 6 * 1024 * 1024
SUBLANES = 8

IN_TILE = 512
ATTN_TILE = 256
ATTN_HEADS_PER_STEP = 4
BACK_TILE = 512
FFN_CHUNK = 512
MM_CHUNK = 512
MOD_TILE = 1024
FCUM_CHUNK = 256


def _const_spec(shape):
    nd = len(shape)
    return pl.BlockSpec(shape, lambda *_: (0,) * nd, pipeline_mode=pl.Buffered(1))


def _params(n_grid):
    return pltpu.CompilerParams(
        dimension_semantics=("arbitrary",) * n_grid, vmem_limit_bytes=VMEM_LIMIT_BYTES)


def _rmsnorm_mod(x, g, shift, scale):
    ms = jnp.mean(x * x, axis=-1, keepdims=True)
    y = x * lax.rsqrt(ms + RMS_EPS)
    return (y * g) * (1.0 + scale) + shift


def _gelu_tanh(x):
    c = math.sqrt(2.0 / math.pi)
    return x * (0.5 + 0.5 * jnp.tanh(x * (c + (0.044715 * c) * (x * x))))


def _softplus(x):
    return jnp.maximum(x, 0.0) + jnp.log1p(jnp.exp(-jnp.abs(x)))


def _chunks(n, width):
    return [(c0, min(width, n - c0)) for c0 in range(0, n, width)]


def _mod_kernel(c_ref, w_ref, b_ref, o_ref):
    c = c_ref[...]
    c_act = (c * jax.nn.sigmoid(c)).astype(BF16)
    o_ref[...] = jnp.dot(c_act, w_ref[...].astype(BF16), preferred_element_type=F32) + b_ref[...]


def _modulation(c, w, b):
    bsz, d = c.shape
    n = w.shape[1]
    tn = min(MOD_TILE, n)
    return pl.pallas_call(
        _mod_kernel,
        grid=(n // tn,),
        in_specs=[pl.BlockSpec((bsz, d), lambda j: (0, 0)),
                  pl.BlockSpec((d, tn), lambda j: (0, j)),
                  pl.BlockSpec((1, tn), lambda j: (0, j))],
        out_specs=pl.BlockSpec((bsz, tn), lambda j: (0, j)),
        out_shape=jax.ShapeDtypeStruct((bsz, n), F32),
        compiler_params=_params(1),
        name="adaln_mod",
    )(c, w, b.reshape(1, n))


def _in_kernel(x_ref, mod_ref, g1_ref, wa_ref, wqkv_ref, wflt_ref, bf_ref, wmg_ref,
               cw_ref, cb_ref, wlru_ref, bla_ref, bli_ref, lam_ref,
               y_ref, q_ref, k_ref, v_ref, lf_ref, g_ref,
               xe_ref, gg_ref, a_ref, u_ref, carry_ref):
    tm, d = x_ref.shape[1], x_ref.shape[2]
    d_rnn = y_ref.shape[2]
    d_attn = N_HEADS * HEAD_DIM

    @pl.when(pl.program_id(1) == 0)
    def _():
        xe_ref[0:SUBLANES, :] = jnp.zeros((SUBLANES, d_rnn), F32)
        carry_ref[...] = jnp.zeros_like(carry_ref)

    h = _rmsnorm_mod(x_ref[0], g1_ref[...], mod_ref[0, :, 0:d], mod_ref[0, :, d:2 * d]).astype(BF16)

    def head_chunk(out_ref, base, scale, c0, cw):
        z = jnp.dot(h, wqkv_ref[:, base + c0:base + c0 + cw], preferred_element_type=F32)
        if scale is not None:
            z = z * scale
        for hh in range(cw // HEAD_DIM):
            out_ref[0, c0 // HEAD_DIM + hh] = z[:, hh * HEAD_DIM:(hh + 1) * HEAD_DIM].astype(BF16)

    def forget_logits():
        fl_t = lax.dot_general(wflt_ref[...], h, (((1,), (1,)), ((), ())), preferred_element_type=F32)
        fl_t = fl_t[0:N_HEADS] + bf_ref[...]
        lf_ref[0] = jnp.minimum(fl_t, 0.0) - jnp.log1p(jnp.exp(-jnp.abs(fl_t)))

    def merge_gate_chunk(c0, cw):
        z = jnp.dot(h, wmg_ref[:, c0:c0 + cw], preferred_element_type=F32)
        g_ref[0, :, c0:c0 + cw] = jax.nn.sigmoid(z).astype(BF16)

    pending = [functools.partial(head_chunk, out_ref, base, scale, c0, cw)
               for out_ref, base, scale in ((q_ref, 0, HEAD_DIM ** -0.5), (k_ref, d_attn, None),
                                            (v_ref, 2 * d_attn, None))
               for c0, cw in _chunks(d_attn, MM_CHUNK)]
    pending += [functools.partial(merge_gate_chunk, c0, cw) for c0, cw in _chunks(2 * d, MM_CHUNK)]
    pending.append(forget_logits)

    for c0, cw in _chunks(d_rnn, MM_CHUNK):
        xe_ref[SUBLANES:SUBLANES + tm, c0:c0 + cw] = jnp.dot(
            h, wa_ref[:, c0:c0 + cw], preferred_element_type=F32)
    for c0, cw in _chunks(d_rnn, MM_CHUNK):
        gg_ref[:, c0:c0 + cw] = _gelu_tanh(jnp.dot(
            h, wa_ref[:, d_rnn + c0:d_rnn + c0 + cw], preferred_element_type=F32))
    sp_scaled = -LRU_C * _softplus(-lam_ref[...])
    for n in range(N_RNN_BLOCKS):
        _rglru_gates(n, xe_ref, cw_ref, cb_ref, wlru_ref, bla_ref, bli_ref, sp_scaled, a_ref, u_ref)
        for job in pending[n * len(pending) // N_RNN_BLOCKS:(n + 1) * len(pending) // N_RNN_BLOCKS]:
            job()
    xe_ref[0:SUBLANES, :] = xe_ref[tm:tm + SUBLANES, :]

    _rglru_scan(a_ref, u_ref, carry_ref)
    for n in range(N_RNN_BLOCKS):
        sl = slice(n * RNN_BLOCK, (n + 1) * RNN_BLOCK)
        y_ref[0, :, sl] = (gg_ref[:, sl] * u_ref[n, 0:tm, :]).astype(BF16)


def _in_projection(x, mod, g1, w_a, w_qkv, w_flt, b_f, w_mg, conv_w, conv_b, w_lru, b_la, b_li, lam):
    bsz, s, d = x.shape
    d_rnn = w_a.shape[1] // 2
    tm = min(IN_TILE, s)
    tok = lambda width: pl.BlockSpec((1, tm, width), lambda b, i: (b, i, 0))
    head = pl.BlockSpec((1, N_HEADS, tm, HEAD_DIM), lambda b, i: (b, 0, i, 0))
    row = _const_spec((1, d_rnn))
    return pl.pallas_call(
        _in_kernel,
        grid=(bsz, s // tm),
        in_specs=[tok(d),
                  pl.BlockSpec((1, 1, N_MOD * d), lambda b, i: (b, 0, 0)),
                  _const_spec((1, d)),
                  _const_spec(w_a.shape), _const_spec(w_qkv.shape), _const_spec(w_flt.shape),
                  _const_spec((N_HEADS, 1)), _const_spec(w_mg.shape),
                  _const_spec(conv_w.shape), row, _const_spec(w_lru.shape), row, row, row],
        out_specs=[tok(d_rnn), head, head, head,
                   pl.BlockSpec((1, N_HEADS, tm), lambda b, i: (b, 0, i)),
                   tok(2 * d)],
        out_shape=[jax.ShapeDtypeStruct((bsz, s, d_rnn), BF16),
                   jax.ShapeDtypeStruct((bsz, N_HEADS, s, HEAD_DIM), BF16),
                   jax.ShapeDtypeStruct((bsz, N_HEADS, s, HEAD_DIM), BF16),
                   jax.ShapeDtypeStruct((bsz, N_HEADS, s, HEAD_DIM), BF16),
                   jax.ShapeDtypeStruct((bsz, N_HEADS, s), F32),
                   jax.ShapeDtypeStruct((bsz, s, 2 * d), BF16)],
        scratch_shapes=[pltpu.VMEM((tm + SUBLANES, d_rnn), F32), pltpu.VMEM((tm, d_rnn), F32)]
        + [pltpu.VMEM((N_RNN_BLOCKS, tm + SUBLANES, RNN_BLOCK), F32)] * 2
        + [pltpu.VMEM((SUBLANES, d_rnn), F32)],
        compiler_params=_params(2),
        name="in_projection",
    )(x, mod.reshape(bsz, 1, N_MOD * d), g1.reshape(1, d), w_a, w_qkv, w_flt,
      b_f.reshape(N_HEADS, 1), w_mg, conv_w, conv_b.reshape(1, d_rnn), w_lru, b_la.reshape(1, d_rnn),
      b_li.reshape(1, d_rnn), lam.reshape(1, d_rnn))


def _fcum_kernel(lf_ref, o_ref):
    rows, s = lf_ref.shape
    cw = min(FCUM_CHUNK, s)
    r_i = lax.broadcasted_iota(jnp.int32, (cw, cw), 0)
    c_i = lax.broadcasted_iota(jnp.int32, (cw, cw), 1)
    tri = (r_i <= c_i).astype(BF16)
    carry = jnp.zeros((rows, 1), F32)
    for c0 in range(0, s, cw):
        v = lf_ref[:, c0:c0 + cw]
        hi = v.astype(BF16)
        r1 = v - hi.astype(F32)
        mid = r1.astype(BF16)
        lo = (r1 - mid.astype(F32)).astype(BF16)
        acc = jnp.dot(hi, tri, preferred_element_type=F32)
        acc = acc + jnp.dot(mid, tri, preferred_element_type=F32)
        acc = acc + jnp.dot(lo, tri, preferred_element_type=F32)
        out = acc + carry
        o_ref[:, c0:c0 + cw] = out
        carry = out[:, cw - 1:cw]


def _forget_cumsum(lf):
    rows, s = lf.shape
    return pl.pallas_call(
        _fcum_kernel,
        out_shape=jax.ShapeDtypeStruct((rows, s), F32),
        compiler_params=pltpu.CompilerParams(vmem_limit_bytes=VMEM_LIMIT_BYTES),
        name="forget_cumsum",
    )(lf)


def _rglru_gates(n, xe_ref, cw_ref, cb_ref, w_ref, ba_ref, bi_ref, sp_scaled, a_ref, u_ref):
    ts = xe_ref.shape[0] - SUBLANES
    kw = cw_ref.shape[0]
    sl = slice(n * RNN_BLOCK, (n + 1) * RNN_BLOCK)
    xc = cb_ref[:, sl]
    for k in range(kw):
        xc = xc + xe_ref[pl.ds(SUBLANES - (kw - 1) + k, ts), sl] * cw_ref[k:k + 1, sl]
    gates = jnp.dot(xc.astype(BF16), w_ref[n], preferred_element_type=F32)
    r = jax.nn.sigmoid(gates[:, 0:RNN_BLOCK] + ba_ref[:, sl])
    i = jax.nn.sigmoid(gates[:, RNN_BLOCK:2 * RNN_BLOCK] + bi_ref[:, sl])
    log_a = r * sp_scaled[:, sl]
    t = jnp.tanh(log_a)
    w = -2.0 * t
    mult = jnp.where(w > 0.0, w * lax.rsqrt(w * (1.0 - t)), 0.0)
    a_ref[n, 0:ts, :] = jnp.exp(log_a)
    u_ref[n, 0:ts, :] = mult * (i * xc)
    a_ref[n, ts:ts + SUBLANES, :] = jnp.ones((SUBLANES, RNN_BLOCK), F32)
    u_ref[n, ts:ts + SUBLANES, :] = jnp.zeros((SUBLANES, RNN_BLOCK), F32)


def _rglru_scan(a_ref, u_ref, carry_ref):
    pitch = (a_ref.shape[1] - SUBLANES) // SUBLANES + 1
    n_unroll = 5 if pitch % 5 == 0 else 1

    def summarize(t, carry):
        idx = pl.ds(t, SUBLANES, stride=pitch)
        out = []
        for n, (h, p) in enumerate(carry):
            a = a_ref[n, idx, :]
            out.append((a * h + u_ref[n, idx, :], a * p))
        return tuple(out)

    ends = lax.fori_loop(
        0, pitch, summarize,
        tuple((jnp.zeros((SUBLANES, RNN_BLOCK), F32), jnp.ones((SUBLANES, RNN_BLOCK), F32))
              for _ in range(N_RNN_BLOCKS)), unroll=n_unroll)

    sub = lax.broadcasted_iota(jnp.int32, (SUBLANES, RNN_BLOCK), 0)
    starts = []
    for n, (h_end, p_end) in enumerate(ends):
        sl = slice(n * RNN_BLOCK, (n + 1) * RNN_BLOCK)
        c = carry_ref[0:1, sl]
        start = jnp.broadcast_to(c, (SUBLANES, RNN_BLOCK))
        for j in range(1, SUBLANES):
            c = h_end[j - 1:j] + p_end[j - 1:j] * c
            start = jnp.where(sub == j, c, start)
        carry_ref[0:1, sl] = h_end[SUBLANES - 1:SUBLANES] + p_end[SUBLANES - 1:SUBLANES] * c
        starts.append(start)

    def rescan(t, hs):
        idx = pl.ds(t, SUBLANES, stride=pitch)
        out = []
        for n, h in enumerate(hs):
            h = a_ref[n, idx, :] * h + u_ref[n, idx, :]
            u_ref[n, idx, :] = h
            out.append(h)
        return tuple(out)

    lax.fori_loop(0, pitch, rescan, tuple(starts), unroll=n_unroll)


def _attn_kernel(q_ref, k_ref, v_ref, f_ref, o_ref):
    hb, s_len = q_ref.shape[1], q_ref.shape[2]
    tq = min(ATTN_TILE, s_len)
    nt = (((1,), (1,)), ((), ()))
    causal = (lax.broadcasted_iota(jnp.int32, (tq, tq), 0)
              >= lax.broadcasted_iota(jnp.int32, (tq, tq), 1))

    def scores(hh, i):
        lo, hi = i * tq, (i + 1) * tq
        q = q_ref[0, hh, lo:hi, :]
        f_q = f_ref[0, 0, hh:hh + 1, lo:hi]
        f_t = jnp.transpose(jnp.broadcast_to(f_q, (HEAD_DIM, tq)))[:, 0:1]
        s_d = lax.dot_general(q, k_ref[0, hh, lo:hi, :], nt, preferred_element_type=F32) - f_q
        s_d = jnp.where(causal, s_d, -jnp.inf)
        m = jnp.max(s_d, axis=-1, keepdims=True)
        s_o = None
        if i > 0:
            s_o = (lax.dot_general(q, k_ref[0, hh, 0:lo, :], nt, preferred_element_type=F32)
                   - f_ref[0, 0, hh:hh + 1, 0:lo])
            m = jnp.maximum(m, jnp.max(s_o, axis=-1, keepdims=True))
        return s_d, s_o, (m + f_t) - f_t

    def output(hh, i, s_d, s_o, shift):
        lo, hi = i * tq, (i + 1) * tq
        p_d = jnp.exp(s_d - shift)
        l = jnp.sum(p_d, axis=-1, keepdims=True)
        acc = jnp.dot(p_d.astype(BF16), v_ref[0, hh, lo:hi, :], preferred_element_type=F32)
        if i > 0:
            p_o = jnp.exp(s_o - shift)
            l = l + jnp.sum(p_o, axis=-1, keepdims=True)
            acc = acc + jnp.dot(p_o.astype(BF16), v_ref[0, hh, 0:lo, :], preferred_element_type=F32)
        o_ref[0, lo:hi, hh * HEAD_DIM:(hh + 1) * HEAD_DIM] = (acc * (1.0 / l)).astype(BF16)

    for i in range(s_len // tq):
        staged = [scores(hh, i) for hh in range(hb)]
        for hh in range(hb):
            output(hh, i, *staged[hh])


def _attention(q, k, v, f_cum):
    bsz, n_heads, s, dh = q.shape
    hb = ATTN_HEADS_PER_STEP
    f4 = f_cum.reshape(bsz, n_heads // hb, hb, s)
    qkv = pl.BlockSpec((1, hb, s, dh), lambda b, g: (b, g, 0, 0))
    return pl.pallas_call(
        _attn_kernel,
        grid=(bsz, n_heads // hb),
        in_specs=[qkv, qkv, qkv, pl.BlockSpec((1, 1, hb, s), lambda b, g: (b, g, 0, 0))],
        out_specs=pl.BlockSpec((1, s, hb * dh), lambda b, g: (b, 0, g)),
        out_shape=jax.ShapeDtypeStruct((bsz, s, n_heads * dh), BF16),
        compiler_params=_params(2),
        name="forgetting_attention",
    )(q, k, v, f4)


def _back_kernel(x_ref, yr_ref, ya_ref, g_ref, mod_ref, g2_ref, wpr_ref, wpa_ref, wo_ref,
                 wup_ref, cw_ref, cb_ref, wdn_ref, modf_ref, gf_ref, o_ref, ge_ref, halo_ref, *, final):
    tm, d = x_ref.shape[1], x_ref.shape[2]
    d_ff = wdn_ref.shape[0]
    kw = cw_ref.shape[0]
    mod = lambda idx: mod_ref[0, :, idx * d:(idx + 1) * d]

    @pl.when(pl.program_id(1) == 0)
    def _():
        halo_ref[...] = jnp.zeros_like(halo_ref)

    pr = jnp.dot(yr_ref[0], wpr_ref[...], preferred_element_type=F32)
    pa = jnp.dot(ya_ref[0], wpa_ref[...], preferred_element_type=F32)
    merged = g_ref[0, :, 0:d].astype(F32) * pr + g_ref[0, :, d:2 * d].astype(F32) * pa
    x1 = x_ref[0] + mod(2) * jnp.dot(merged.astype(BF16), wo_ref[...], preferred_element_type=F32)

    h2 = _rmsnorm_mod(x1, g2_ref[...], mod(3), mod(4)).astype(BF16)
    y = jnp.zeros((tm, d), F32)
    for c0, cw in _chunks(d_ff, FFN_CHUNK):
        cs = slice(c0, c0 + cw)
        gfc = jnp.dot(h2, wup_ref[:, cs], preferred_element_type=F32)
        ufc = jnp.dot(h2, wup_ref[:, d_ff + c0:d_ff + c0 + cw], preferred_element_type=F32)
        ge_ref[0:SUBLANES, 0:cw] = halo_ref[:, cs]
        ge_ref[SUBLANES:SUBLANES + tm, 0:cw] = gfc
        conv = cb_ref[:, cs]
        for k in range(kw - 1):
            conv = conv + ge_ref[pl.ds(SUBLANES - (kw - 1) + k, tm), 0:cw] * cw_ref[k:k + 1, cs]
        conv = conv + gfc * cw_ref[kw - 1:kw, cs]
        halo_ref[:, cs] = ge_ref[tm:tm + SUBLANES, 0:cw]
        act = (_gelu_tanh(conv) * ufc).astype(BF16)
        y = y + jnp.dot(act, wdn_ref[cs, :], preferred_element_type=F32)

    x2 = x1 + mod(5) * y
    if final:
        x2 = _rmsnorm_mod(x2, gf_ref[...], modf_ref[0, :, 0:d], modf_ref[0, :, d:2 * d])
    o_ref[0] = x2


def _back_half(x, y_rnn, y_attn, g, mod, g2, w_pr, w_pa, w_o, w_up, conv_w, conv_b, w_dn,
               mod_f, g_f, final):
    bsz, s, d = x.shape
    d_ff = w_dn.shape[0]
    tm = min(BACK_TILE, s)
    tok = lambda width: pl.BlockSpec((1, tm, width), lambda b, i: (b, i, 0))
    return pl.pallas_call(
        functools.partial(_back_kernel, final=final),
        grid=(bsz, s // tm),
        in_specs=[tok(d), tok(y_rnn.shape[2]), tok(y_attn.shape[2]), tok(2 * d),
                  pl.BlockSpec((1, 1, N_MOD * d), lambda b, i: (b, 0, 0)),
                  _const_spec((1, d)),
                  _const_spec(w_pr.shape), _const_spec(w_pa.shape), _const_spec(w_o.shape),
                  _const_spec(w_up.shape), _const_spec(conv_w.shape), _const_spec((1, d_ff)),
                  _const_spec(w_dn.shape),
                  pl.BlockSpec((1, 1, 2 * d), lambda b, i: (b, 0, 0)),
                  _const_spec((1, d))],
        out_specs=tok(d),
        out_shape=jax.ShapeDtypeStruct((bsz, s, d), F32),
        scratch_shapes=[pltpu.VMEM((tm + SUBLANES, FFN_CHUNK), F32),
                        pltpu.VMEM((SUBLANES, d_ff), F32)],
        compiler_params=_params(2),
        name="merge_ffn",
    )(x, y_rnn, y_attn, g, mod.reshape(bsz, 1, N_MOD * d), g2.reshape(1, d), w_pr, w_pa, w_o,
      w_up, conv_w, conv_b.reshape(1, d_ff), w_dn, mod_f.reshape(bsz, 1, 2 * d), g_f.reshape(1, d))


def kernel(x, c, w_ada, b_ada, g_norm1, w_in, w_rnn_conv, b_rnn_conv, w_lru_a, b_lru_a, w_lru_i, b_lru_i, lru_lambda, b_fgate, w_proj_rnn, w_proj_attn, w_out, g_norm2, w_ffn_up, w_ffn_conv, b_ffn_conv, w_ffn_down, w_ada_final, b_ada_final, g_final):
    bsz, s, d = x.shape
    depth = w_ada.shape[0]
    d_rnn = w_rnn_conv.shape[2]
    d_attn = N_HEADS * HEAD_DIM
    mod_f = _modulation(c, w_ada_final, b_ada_final)
    for l in range(depth):
        mod = _modulation(c, w_ada[l], b_ada[l])
        o_qkv = 2 * d_rnn
        o_fl = o_qkv + 3 * d_attn
        o_mg = o_fl + N_HEADS
        w_a = w_in[l][:, :o_qkv].astype(BF16)
        w_qkv = w_in[l][:, o_qkv:o_fl].astype(BF16)
        w_flt = jnp.pad(w_in[l][:, o_fl:o_mg].T, ((0, 2 * SUBLANES - N_HEADS), (0, 0))).astype(BF16)
        w_mg = w_in[l][:, o_mg:].astype(BF16)

        y_rnn, q, k, v, lf, g = _in_projection(
            x, mod, g_norm1[l], w_a, w_qkv, w_flt, b_fgate[l], w_mg, w_rnn_conv[l], b_rnn_conv[l],
            jnp.concatenate([w_lru_a[l], w_lru_i[l]], axis=-1).astype(BF16), b_lru_a[l], b_lru_i[l],
            lru_lambda[l])
        f_cum = _forget_cumsum(lf.reshape(bsz * N_HEADS, s)).reshape(bsz, N_HEADS, s)
        y_attn = _attention(q, k, v, f_cum)
        x = _back_half(x, y_rnn, y_attn, g, mod, g_norm2[l], w_proj_rnn[l].astype(BF16),
                       w_proj_attn[l].astype(BF16), w_out[l].astype(BF16), w_ffn_up[l].astype(BF16),
                       w_ffn_conv[l], b_ffn_conv[l], w_ffn_down[l].astype(BF16), mod_f, g_final,
                       final=(l == depth - 1))
    return x
```

```python
import functools
import math

import jax
import jax.numpy as jnp
from jax import lax
from jax.experimental import pallas as pl
from jax.experimental.pallas import tpu as pltpu

F32 = jnp.float32
BF16 = jnp.bfloat16

N_RNN_BLOCKS = 10
RNN_BLOCK = 128
LRU_C = 8.0
N_HEADS = 8
HEAD_DIM = 128
RMS_EPS = 1e-6
N_MOD = 6
LOG2E = math.log2(math.e)

V7X_VMEM_BYTES = 64 * 1024 * 1024
VMEM_LIMIT_BYTES = V7X_VMEM_BYTES - 6 * 1024 * 1024
SUBLANES = 8

IN_TILE = 512
ATTN_TILE = 256
ATTN_HEADS_PER_STEP = 4
BACK_TILE = 512
FFN_CHUNK = 512
MM_CHUNK = 512
MOD_TILE = 1024
FCUM_CHUNK = 256


def _const_spec(shape):
    nd = len(shape)
    return pl.BlockSpec(shape, lambda *_: (0,) * nd, pipeline_mode=pl.Buffered(1))


def _params(n_grid):
    return pltpu.CompilerParams(
        dimension_semantics=("arbitrary",) * n_grid, vmem_limit_bytes=VMEM_LIMIT_BYTES)


def _rmsnorm_mod(x, g, shift, scale):
    ms = jnp.mean(x * x, axis=-1, keepdims=True)
    y = x * lax.rsqrt(ms + RMS_EPS)
    return (y * g) * (1.0 + scale) + shift


def _gelu_tanh(x):
    c = math.sqrt(2.0 / math.pi)
    return x * (0.5 + 0.5 * jnp.tanh(x * (c + (0.044715 * c) * (x * x))))


def _softplus(x):
    return jnp.maximum(x, 0.0) + jnp.log1p(jnp.exp(-jnp.abs(x)))


def _chunks(n, width):
    return [(c0, min(width, n - c0)) for c0 in range(0, n, width)]


def _mod_kernel(c_ref, w_ref, b_ref, o_ref):
    c = c_ref[...]
    c_act = (c * jax.nn.sigmoid(c)).astype(BF16)
    o_ref[...] = jnp.dot(c_act, w_ref[...].astype(BF16), preferred_element_type=F32) + b_ref[...]


def _modulation(c, w, b):
    bsz, d = c.shape
    n = w.shape[1]
    tn = min(MOD_TILE, n)
    return pl.pallas_call(
        _mod_kernel,
        grid=(n // tn,),
        in_specs=[pl.BlockSpec((bsz, d), lambda j: (0, 0)),
                  pl.BlockSpec((d, tn), lambda j: (0, j)),
                  pl.BlockSpec((1, tn), lambda j: (0, j))],
        out_specs=pl.BlockSpec((bsz, tn), lambda j: (0, j)),
        out_shape=jax.ShapeDtypeStruct((bsz, n), F32),
        compiler_params=_params(1),
        name="adaln_mod",
    )(c, w, b.reshape(1, n))


def _in_kernel(x_ref, mod_ref, g1_ref, wa_ref, wqkv_ref, wflt_ref, bf_ref, wmg_ref,
               cw_ref, cb_ref, wlru_ref, bla_ref, bli_ref, lam_ref,
               y_ref, q_ref, k_ref, v_ref, lf_ref, g_ref,
               xe_ref, gg_ref, a_ref, u_ref, carry_ref):
    tm, d = x_ref.shape[1], x_ref.shape[2]
    d_rnn = y_ref.shape[2]
    d_attn = N_HEADS * HEAD_DIM

    @pl.when(pl.program_id(1) == 0)
    def _():
        xe_ref[0:SUBLANES, :] = jnp.zeros((SUBLANES, d_rnn), F32)
        carry_ref[...] = jnp.zeros_like(carry_ref)

    h = _rmsnorm_mod(x_ref[0], g1_ref[...], mod_ref[0, :, 0:d], mod_ref[0, :, d:2 * d]).astype(BF16)

    def head_chunk(out_ref, base, scale, c0, cw):
        z = jnp.dot(h, wqkv_ref[:, base + c0:base + c0 + cw], preferred_element_type=F32)
        if scale is not None:
            z = z * scale
        for hh in range(cw // HEAD_DIM):
            out_ref[0, c0 // HEAD_DIM + hh] = z[:, hh * HEAD_DIM:(hh + 1) * HEAD_DIM].astype(BF16)

    def forget_logits():
        fl_t = lax.dot_general(wflt_ref[...], h, (((1,), (1,)), ((), ())), preferred_element_type=F32)
        fl_t = fl_t[0:N_HEADS] + bf_ref[...]
        lf_ref[0] = jnp.minimum(fl_t, 0.0) - jnp.log1p(jnp.exp(-jnp.abs(fl_t)))

    def merge_gate_chunk(c0, cw):
        z = jnp.dot(h, wmg_ref[:, c0:c0 + cw], preferred_element_type=F32)
        g_ref[0, :, c0:c0 + cw] = jax.nn.sigmoid(z).astype(BF16)

    pending = [functools.partial(head_chunk, out_ref, base, scale, c0, cw)
               for out_ref, base, scale in ((q_ref, 0, HEAD_DIM ** -0.5 * LOG2E), (k_ref, d_attn, None),
                                            (v_ref, 2 * d_attn, None))
               for c0, cw in _chunks(d_attn, MM_CHUNK)]
    pending += [functools.partial(merge_gate_chunk, c0, cw) for c0, cw in _chunks(2 * d, MM_CHUNK)]
    pending.append(forget_logits)

    for c0, cw in _chunks(d_rnn, MM_CHUNK):
        xe_ref[SUBLANES:SUBLANES + tm, c0:c0 + cw] = jnp.dot(
            h, wa_ref[:, c0:c0 + cw], preferred_element_type=F32)
    for c0, cw in _chunks(d_rnn, MM_CHUNK):
        gg_ref[:, c0:c0 + cw] = _gelu_tanh(jnp.dot(
            h, wa_ref[:, d_rnn + c0:d_rnn + c0 + cw], preferred_element_type=F32))
    sp_scaled = -LRU_C * _softplus(-lam_ref[...])
    for n in range(N_RNN_BLOCKS):
        _rglru_gates(n, xe_ref, cw_ref, cb_ref, wlru_ref, bla_ref, bli_ref, sp_scaled, a_ref, u_ref)
        for job in pending[n * len(pending) // N_RNN_BLOCKS:(n + 1) * len(pending) // N_RNN_BLOCKS]:
            job()
    xe_ref[0:SUBLANES, :] = xe_ref[tm:tm + SUBLANES, :]

    _rglru_scan(a_ref, u_ref, carry_ref)
    for n in range(N_RNN_BLOCKS):
        sl = slice(n * RNN_BLOCK, (n + 1) * RNN_BLOCK)
        y_ref[0, :, sl] = (gg_ref[:, sl] * u_ref[n, 0:tm, :]).astype(BF16)


def _in_projection(x, mod, g1, w_a, w_qkv, w_flt, b_f, w_mg, conv_w, conv_b, w_lru, b_la, b_li, lam):
    bsz, s, d = x.shape
    d_rnn = w_a.shape[1] // 2
    tm = min(IN_TILE, s)
    tok = lambda width: pl.BlockSpec((1, tm, width), lambda b, i: (b, i, 0))
    head = pl.BlockSpec((1, N_HEADS, tm, HEAD_DIM), lambda b, i: (b, 0, i, 0))
    row = _const_spec((1, d_rnn))
    return pl.pallas_call(
        _in_kernel,
        grid=(bsz, s // tm),
        in_specs=[tok(d),
                  pl.BlockSpec((1, 1, N_MOD * d), lambda b, i: (b, 0, 0)),
                  _const_spec((1, d)),
                  _const_spec(w_a.shape), _const_spec(w_qkv.shape), _const_spec(w_flt.shape),
                  _const_spec((N_HEADS, 1)), _const_spec(w_mg.shape),
                  _const_spec(conv_w.shape), row, _const_spec(w_lru.shape), row, row, row],
        out_specs=[tok(d_rnn), head, head, head,
                   pl.BlockSpec((1, N_HEADS, tm), lambda b, i: (b, 0, i)),
                   tok(2 * d)],
        out_shape=[jax.ShapeDtypeStruct((bsz, s, d_rnn), BF16),
                   jax.ShapeDtypeStruct((bsz, N_HEADS, s, HEAD_DIM), BF16),
                   jax.ShapeDtypeStruct((bsz, N_HEADS, s, HEAD_DIM), BF16),
                   jax.ShapeDtypeStruct((bsz, N_HEADS, s, HEAD_DIM), BF16),
                   jax.ShapeDtypeStruct((bsz, N_HEADS, s), F32),
                   jax.ShapeDtypeStruct((bsz, s, 2 * d), BF16)],
        scratch_shapes=[pltpu.VMEM((tm + SUBLANES, d_rnn), F32), pltpu.VMEM((tm, d_rnn), F32)]
        + [pltpu.VMEM((N_RNN_BLOCKS, tm + SUBLANES, RNN_BLOCK), F32)] * 2
        + [pltpu.VMEM((SUBLANES, d_rnn), F32)],
        compiler_params=_params(2),
        name="in_projection",
    )(x, mod.reshape(bsz, 1, N_MOD * d), g1.reshape(1, d), w_a, w_qkv, w_flt,
      b_f.reshape(N_HEADS, 1), w_mg, conv_w, conv_b.reshape(1, d_rnn), w_lru, b_la.reshape(1, d_rnn),
      b_li.reshape(1, d_rnn), lam.reshape(1, d_rnn))


def _fcum_kernel(lf_ref, o_ref):
    rows, s = lf_ref.shape
    cw = min(FCUM_CHUNK, s)
    r_i = lax.broadcasted_iota(jnp.int32, (cw, cw), 0)
    c_i = lax.broadcasted_iota(jnp.int32, (cw, cw), 1)
    tri = (r_i <= c_i).astype(BF16)
    carry = jnp.zeros((rows, 1), F32)
    for c0 in range(0, s, cw):
        v = lf_ref[:, c0:c0 + cw]
        hi = v.astype(BF16)
        r1 = v - hi.astype(F32)
        mid = r1.astype(BF16)
        lo = (r1 - mid.astype(F32)).astype(BF16)
        acc = jnp.dot(hi, tri, preferred_element_type=F32)
        acc = acc + jnp.dot(mid, tri, preferred_element_type=F32)
        acc = acc + jnp.dot(lo, tri, preferred_element_type=F32)
        out = acc + carry
        o_ref[:, c0:c0 + cw] = out * LOG2E
        carry = out[:, cw - 1:cw]


def _forget_cumsum(lf):
    rows, s = lf.shape
    return pl.pallas_call(
        _fcum_kernel,
        out_shape=jax.ShapeDtypeStruct((rows, s), F32),
        compiler_params=pltpu.CompilerParams(vmem_limit_bytes=VMEM_LIMIT_BYTES),
        name="forget_cumsum",
    )(lf)


def _rglru_gates(n, xe_ref, cw_ref, cb_ref, w_ref, ba_ref, bi_ref, sp_scaled, a_ref, u_ref):
    ts = xe_ref.shape[0] - SUBLANES
    kw = cw_ref.shape[0]
    sl = slice(n * RNN_BLOCK, (n + 1) * RNN_BLOCK)
    xc = cb_ref[:, sl]
    for k in range(kw):
        xc = xc + xe_ref[pl.ds(SUBLANES - (kw - 1) + k, ts), sl] * cw_ref[k:k + 1, sl]
    gates = jnp.dot(xc.astype(BF16), w_ref[n], preferred_element_type=F32)
    r = jax.nn.sigmoid(gates[:, 0:RNN_BLOCK] + ba_ref[:, sl])
    i = jax.nn.sigmoid(gates[:, RNN_BLOCK:2 * RNN_BLOCK] + bi_ref[:, sl])
    log_a = r * sp_scaled[:, sl]
    t = jnp.tanh(log_a)
    w = -2.0 * t
    mult = jnp.where(w > 0.0, w * lax.rsqrt(w * (1.0 - t)), 0.0)
    a_ref[n, 0:ts, :] = jnp.exp(log_a)
    u_ref[n, 0:ts, :] = mult * (i * xc)
    a_ref[n, ts:ts + SUBLANES, :] = jnp.ones((SUBLANES, RNN_BLOCK), F32)
    u_ref[n, ts:ts + SUBLANES, :] = jnp.zeros((SUBLANES, RNN_BLOCK), F32)


def _rglru_scan(a_ref, u_ref, carry_ref):
    pitch = (a_ref.shape[1] - SUBLANES) // SUBLANES + 1
    n_unroll = 5 if pitch % 5 == 0 else 1

    def summarize(t, carry):
        idx = pl.ds(t, SUBLANES, stride=pitch)
        out = []
        for n, (h, p) in enumerate(carry):
            a = a_ref[n, idx, :]
            out.append((a * h + u_ref[n, idx, :], a * p))
        return tuple(out)

    ends = lax.fori_loop(
        0, pitch, summarize,
        tuple((jnp.zeros((SUBLANES, RNN_BLOCK), F32), jnp.ones((SUBLANES, RNN_BLOCK), F32))
              for _ in range(N_RNN_BLOCKS)), unroll=n_unroll)

    sub = lax.broadcasted_iota(jnp.int32, (SUBLANES, RNN_BLOCK), 0)
    starts = []
    for n, (h_end, p_end) in enumerate(ends):
        sl = slice(n * RNN_BLOCK, (n + 1) * RNN_BLOCK)
        c = carry_ref[0:1, sl]
        start = jnp.broadcast_to(c, (SUBLANES, RNN_BLOCK))
        for j in range(1, SUBLANES):
            c = h_end[j - 1:j] + p_end[j - 1:j] * c
            start = jnp.where(sub == j, c, start)
        carry_ref[0:1, sl] = h_end[SUBLANES - 1:SUBLANES] + p_end[SUBLANES - 1:SUBLANES] * c
        starts.append(start)

    def rescan(t, hs):
        idx = pl.ds(t, SUBLANES, stride=pitch)
        out = []
        for n, h in enumerate(hs):
            h = a_ref[n, idx, :] * h + u_ref[n, idx, :]
            u_ref[n, idx, :] = h
            out.append(h)
        return tuple(out)

    lax.fori_loop(0, pitch, rescan, tuple(starts), unroll=n_unroll)


def _attn_kernel(q_ref, k_ref, v_ref, f_ref, o_ref):
    hb, s_len = q_ref.shape[1], q_ref.shape[2]
    tq = min(ATTN_TILE, s_len)
    nt = (((1,), (1,)), ((), ()))
    causal = (lax.broadcasted_iota(jnp.int32, (tq, tq), 0)
              >= lax.broadcasted_iota(jnp.int32, (tq, tq), 1))

    def scores(hh, i):
        lo, hi = i * tq, (i + 1) * tq
        q = q_ref[0, hh, lo:hi, :]
        f_q = f_ref[0, 0, hh:hh + 1, lo:hi]
        f_t = jnp.transpose(jnp.broadcast_to(f_q, (HEAD_DIM, tq)))[:, 0:1]
        s_d = lax.dot_general(q, k_ref[0, hh, lo:hi, :], nt, preferred_element_type=F32) - f_q
        s_d = jnp.where(causal, s_d, -jnp.inf)
        m = jnp.max(s_d, axis=-1, keepdims=True)
        s_o = None
        if i > 0:
            s_o = (lax.dot_general(q, k_ref[0, hh, 0:lo, :], nt, preferred_element_type=F32)
                   - f_ref[0, 0, hh:hh + 1, 0:lo])
            m = jnp.maximum(m, jnp.max(s_o, axis=-1, keepdims=True))
        return s_d, s_o, (m + f_t) - f_t

    def output(hh, i, s_d, s_o, shift):
        lo, hi = i * tq, (i + 1) * tq
        p_d = jnp.exp2(s_d - shift)
        l = jnp.sum(p_d, axis=-1, keepdims=True)
        acc = jnp.dot(p_d.astype(BF16), v_ref[0, hh, lo:hi, :], preferred_element_type=F32)
        if i > 0:
            p_o = jnp.exp2(s_o - shift)
            l = l + jnp.sum(p_o, axis=-1, keepdims=True)
            acc = acc + jnp.dot(p_o.astype(BF16), v_ref[0, hh, 0:lo, :], preferred_element_type=F32)
        o_ref[0, lo:hi, hh * HEAD_DIM:(hh + 1) * HEAD_DIM] = (acc * (1.0 / l)).astype(BF16)

    for i in range(s_len // tq):
        staged = [scores(hh, i) for hh in range(hb)]
        for hh in range(hb):
            output(hh, i, *staged[hh])


def _attention(q, k, v, f_cum):
    bsz, n_heads, s, dh = q.shape
    hb = ATTN_HEADS_PER_STEP
    f4 = f_cum.reshape(bsz, n_heads // hb, hb, s)
    qkv = pl.BlockSpec((1, hb, s, dh), lambda b, g: (b, g, 0, 0))
    return pl.pallas_call(
        _attn_kernel,
        grid=(bsz, n_heads // hb),
        in_specs=[qkv, qkv, qkv, pl.BlockSpec((1, 1, hb, s), lambda b, g: (b, g, 0, 0))],
        out_specs=pl.BlockSpec((1, s, hb * dh), lambda b, g: (b, 0, g)),
        out_shape=jax.ShapeDtypeStruct((bsz, s, n_heads * dh), BF16),
        compiler_params=_params(2),
        name="forgetting_attention",
    )(q, k, v, f4)


def _back_kernel(x_ref, yr_ref, ya_ref, g_ref, mod_ref, g2_ref, wpr_ref, wpa_ref, wo_ref,
                 wup_ref, cw_ref, cb_ref, wdn_ref, modf_ref, gf_ref, o_ref, ge_ref, halo_ref, *, final):
    tm, d = x_ref.shape[1], x_ref.shape[2]
    d_ff = wdn_ref.shape[0]
    kw = cw_ref.shape[0]
    mod = lambda idx: mod_ref[0, :, idx * d:(idx + 1) * d]

    @pl.when(pl.program_id(1) == 0)
    def _():
        halo_ref[...] = jnp.zeros_like(halo_ref)

    pr = jnp.dot(yr_ref[0], wpr_ref[...], preferred_element_type=F32)
    pa = jnp.dot(ya_ref[0], wpa_ref[...], preferred_element_type=F32)
    merged = g_ref[0, :, 0:d].astype(F32) * pr + g_ref[0, :, d:2 * d].astype(F32) * pa
    x1 = x_ref[0] + mod(2) * jnp.dot(merged.astype(BF16), wo_ref[...], preferred_element_type=F32)

    h2 = _rmsnorm_mod(x1, g2_ref[...], mod(3), mod(4)).astype(BF16)
    chunks = _chunks(d_ff, FFN_CHUNK)

    def up(c):
        c0, cw = chunks[c]
        return (jnp.dot(h2, wup_ref[:, c0:c0 + cw], preferred_element_type=F32),
                jnp.dot(h2, wup_ref[:, d_ff + c0:d_ff + c0 + cw], preferred_element_type=F32))

    def activate(c, gfc, ufc):
        c0, cw = chunks[c]
        cs = slice(c0, c0 + cw)
        ge = ge_ref.at[c % 2]
        ge[0:SUBLANES, 0:cw] = halo_ref[:, cs]
        ge[SUBLANES:SUBLANES + tm, 0:cw] = gfc
        conv = cb_ref[:, cs]
        for k in range(kw - 1):
            conv = conv + ge[pl.ds(SUBLANES - (kw - 1) + k, tm), 0:cw] * cw_ref[k:k + 1, cs]
        conv = conv + gfc * cw_ref[kw - 1:kw, cs]
        halo_ref[:, cs] = ge[tm:tm + SUBLANES, 0:cw]
        return (_gelu_tanh(conv) * ufc).astype(BF16)

    def down(c, act):
        c0, cw = chunks[c]
        return jnp.dot(act, wdn_ref[c0:c0 + cw, :], preferred_element_type=F32)

    n = len(chunks)
    y = None
    nxt = up(0)
    act_prev = None
    for c in range(n):
        cur, nxt = nxt, (up(c + 1) if c + 1 < n else None)
        act = activate(c, *cur)
        if act_prev is not None:
            part = down(c - 1, act_prev)
            y = part if y is None else y + part
        act_prev = act
    part = down(n - 1, act_prev)
    y = part if y is None else y + part

    x2 = x1 + mod(5) * y
    if final:
        x2 = _rmsnorm_mod(x2, gf_ref[...], modf_ref[0, :, 0:d], modf_ref[0, :, d:2 * d])
    o_ref[0] = x2


def _back_half(x, y_rnn, y_attn, g, mod, g2, w_pr, w_pa, w_o, w_up, conv_w, conv_b, w_dn,
               mod_f, g_f, final):
    bsz, s, d = x.shape
    d_ff = w_dn.shape[0]
    tm = min(BACK_TILE, s)
    tok = lambda width: pl.BlockSpec((1, tm, width), lambda b, i: (b, i, 0))
    return pl.pallas_call(
        functools.partial(_back_kernel, final=final),
        grid=(bsz, s // tm),
        in_specs=[tok(d), tok(y_rnn.shape[2]), tok(y_attn.shape[2]), tok(2 * d),
                  pl.BlockSpec((1, 1, N_MOD * d), lambda b, i: (b, 0, 0)),
                  _const_spec((1, d)),
                  _const_spec(w_pr.shape), _const_spec(w_pa.shape), _const_spec(w_o.shape),
                  _const_spec(w_up.shape), _const_spec(conv_w.shape), _const_spec((1, d_ff)),
                  _const_spec(w_dn.shape),
                  pl.BlockSpec((1, 1, 2 * d), lambda b, i: (b, 0, 0)),
                  _const_spec((1, d))],
        out_specs=tok(d),
        out_shape=jax.ShapeDtypeStruct((bsz, s, d), F32),
        scratch_shapes=[pltpu.VMEM((2, tm + SUBLANES, FFN_CHUNK), F32),
                        pltpu.VMEM((SUBLANES, d_ff), F32)],
        compiler_params=_params(2),
        name="merge_ffn",
    )(x, y_rnn, y_attn, g, mod.reshape(bsz, 1, N_MOD * d), g2.reshape(1, d), w_pr, w_pa, w_o,
      w_up, conv_w, conv_b.reshape(1, d_ff), w_dn, mod_f.reshape(bsz, 1, 2 * d), g_f.reshape(1, d))


def kernel(x, c, w_ada, b_ada, g_norm1, w_in, w_rnn_conv, b_rnn_conv, w_lru_a, b_lru_a, w_lru_i, b_lru_i, lru_lambda, b_fgate, w_proj_rnn, w_proj_attn, w_out, g_norm2, w_ffn_up, w_ffn_conv, b_ffn_conv, w_ffn_down, w_ada_final, b_ada_final, g_final):
    bsz, s, d = x.shape
    depth = w_ada.shape[0]
    d_rnn = w_rnn_conv.shape[2]
    d_attn = N_HEADS * HEAD_DIM
    mod_f = _modulation(c, w_ada_final, b_ada_final)
    for l in range(depth):
        mod = _modulation(c, w_ada[l], b_ada[l])
        o_qkv = 2 * d_rnn
        o_fl = o_qkv + 3 * d_attn
        o_mg = o_fl + N_HEADS
        w_a = w_in[l][:, :o_qkv].astype(BF16)
        w_qkv = w_in[l][:, o_qkv:o_fl].astype(BF16)
        w_flt = jnp.pad(w_in[l][:, o_fl:o_mg].T, ((0, 2 * SUBLANES - N_HEADS), (0, 0))).astype(BF16)
        w_mg = w_in[l][:, o_mg:].astype(BF16)

        y_rnn, q, k, v, lf, g = _in_projection(
            x, mod, g_norm1[l], w_a, w_qkv, w_flt, b_fgate[l], w_mg, w_rnn_conv[l], b_rnn_conv[l],
            jnp.concatenate([w_lru_a[l], w_lru_i[l]], axis=-1).astype(BF16), b_lru_a[l], b_lru_i[l],
            lru_lambda[l])
        f_cum = _forget_cumsum(lf.reshape(bsz * N_HEADS, s)).reshape(bsz, N_HEADS, s)
        y_attn = _attention(q, k, v, f_cum)
        x = _back_half(x, y_rnn, y_attn, g, mod, g_norm2[l], w_proj_rnn[l].astype(BF16),
                       w_proj_attn[l].astype(BF16), w_out[l].astype(BF16), w_ffn_up[l].astype(BF16),
                       w_ffn_conv[l], b_ffn_conv[l], w_ffn_down[l].astype(BF16), mod_f, g_final,
                       final=(l == depth - 1))
    return x
```

```python
import functools
import math

import jax
import jax.numpy as jnp
from jax import lax
from jax.experimental import pallas as pl
from jax.experimental.pallas import tpu as pltpu

F32 = jnp.float32
BF16 = jnp.bfloat16

N_RNN_BLOCKS = 10
RNN_BLOCK = 128
LRU_C = 8.0
N_HEADS = 8
HEAD_DIM = 128
RMS_EPS = 1e-6
N_MOD = 6
LOG2E = math.log2(math.e)

V7X_VMEM_BYTES = 64 * 1024 * 1024
VMEM_LIMIT_BYTES = V7X_VMEM_BYTES - 6 * 1024 * 1024
SUBLANES = 8

IN_TILE = 512
ATTN_TILE = 256
ATTN_HEADS_PER_STEP = 4
BACK_TILE = 512
FFN_CHUNK = 1024
MM_CHUNK = 512
MOD_TILE = 1024
FCUM_CHUNK = 256


def _const_spec(shape):
    nd = len(shape)
    return pl.BlockSpec(shape, lambda *_: (0,) * nd, pipeline_mode=pl.Buffered(1))


def _params(n_grid):
    return pltpu.CompilerParams(
        dimension_semantics=("arbitrary",) * n_grid, vmem_limit_bytes=VMEM_LIMIT_BYTES)


def _rmsnorm_mod(x, g, shift, scale):
    ms = jnp.mean(x * x, axis=-1, keepdims=True)
    y = x * lax.rsqrt(ms + RMS_EPS)
    return y * (g * (1.0 + scale)) + shift


def _gelu_tanh(x):
    c = math.sqrt(2.0 / math.pi)
    return x * (0.5 + 0.5 * jnp.tanh(x * (c + (0.044715 * c) * (x * x))))


def _softplus(x):
    return jnp.maximum(x, 0.0) + jnp.log1p(jnp.exp(-jnp.abs(x)))


def _chunks(n, width):
    return [(c0, min(width, n - c0)) for c0 in range(0, n, width)]


def _mod_kernel(c_ref, w_ref, b_ref, o_ref):
    c = c_ref[...]
    c_act = (c * jax.nn.sigmoid(c)).astype(BF16)
    o_ref[...] = jnp.dot(c_act, w_ref[...].astype(BF16), preferred_element_type=F32) + b_ref[...]


def _modulation(c, w, b):
    bsz, d = c.shape
    n = w.shape[1]
    tn = min(MOD_TILE, n)
    return pl.pallas_call(
        _mod_kernel,
        grid=(n // tn,),
        in_specs=[pl.BlockSpec((bsz, d), lambda j: (0, 0)),
                  pl.BlockSpec((d, tn), lambda j: (0, j)),
                  pl.BlockSpec((1, tn), lambda j: (0, j))],
        out_specs=pl.BlockSpec((bsz, tn), lambda j: (0, j)),
        out_shape=jax.ShapeDtypeStruct((bsz, n), F32),
        compiler_params=_params(1),
        name="adaln_mod",
    )(c, w, b.reshape(1, n))


def _in_kernel(x_ref, mod_ref, g1_ref, wa_ref, wqkv_ref, wflt_ref, bf_ref, wmg_ref,
               cw_ref, cb_ref, wlru_ref, bla_ref, bli_ref, lam_ref,
               y_ref, q_ref, k_ref, v_ref, lf_ref, g_ref,
               xe_ref, gg_ref, a_ref, u_ref, carry_ref):
    tm, d = x_ref.shape[1], x_ref.shape[2]
    d_rnn = y_ref.shape[2]
    d_attn = N_HEADS * HEAD_DIM

    @pl.when(pl.program_id(1) == 0)
    def _():
        xe_ref[0:SUBLANES, :] = jnp.zeros((SUBLANES, d_rnn), F32)
        carry_ref[...] = jnp.zeros_like(carry_ref)

    h = _rmsnorm_mod(x_ref[0], g1_ref[...], mod_ref[0, :, 0:d], mod_ref[0, :, d:2 * d]).astype(BF16)

    def head_chunk(out_ref, base, scale, c0, cw):
        z = jnp.dot(h, wqkv_ref[:, base + c0:base + c0 + cw], preferred_element_type=F32)
        if scale is not None:
            z = z * scale
        for hh in range(cw // HEAD_DIM):
            out_ref[0, c0 // HEAD_DIM + hh] = z[:, hh * HEAD_DIM:(hh + 1) * HEAD_DIM].astype(BF16)

    def forget_logits():
        fl_t = lax.dot_general(wflt_ref[...], h, (((1,), (1,)), ((), ())), preferred_element_type=F32)
        fl_t = fl_t[0:N_HEADS] + bf_ref[...]
        lf_ref[0] = jnp.minimum(fl_t, 0.0) - jnp.log1p(jnp.exp(-jnp.abs(fl_t)))

    def merge_gate_chunk(c0, cw):
        z = jnp.dot(h, wmg_ref[:, c0:c0 + cw], preferred_element_type=F32)
        g_ref[0, :, c0:c0 + cw] = jax.nn.sigmoid(z).astype(BF16)

    pending = [functools.partial(head_chunk, out_ref, base, scale, c0, cw)
               for out_ref, base, scale in ((q_ref, 0, HEAD_DIM ** -0.5 * LOG2E), (k_ref, d_attn, None),
                                            (v_ref, 2 * d_attn, None))
               for c0, cw in _chunks(d_attn, MM_CHUNK)]
    pending += [functools.partial(merge_gate_chunk, c0, cw) for c0, cw in _chunks(2 * d, MM_CHUNK)]
    pending.append(forget_logits)

    def gelu_gate_chunk(c0, cw):
        gg_ref[:, c0:c0 + cw] = _gelu_tanh(jnp.dot(
            h, wa_ref[:, d_rnn + c0:d_rnn + c0 + cw], preferred_element_type=F32))

    pending += [functools.partial(gelu_gate_chunk, c0, cw) for c0, cw in _chunks(d_rnn, MM_CHUNK)]

    for c0, cw in _chunks(d_rnn, MM_CHUNK):
        xe_ref[SUBLANES:SUBLANES + tm, c0:c0 + cw] = jnp.dot(
            h, wa_ref[:, c0:c0 + cw], preferred_element_type=F32)
    sp_scaled = -LRU_C * _softplus(-lam_ref[...])
    for n in range(N_RNN_BLOCKS):
        _rglru_gates(n, xe_ref, cw_ref, cb_ref, wlru_ref, bla_ref, bli_ref, sp_scaled, a_ref, u_ref)
        for job in pending[n * len(pending) // N_RNN_BLOCKS:(n + 1) * len(pending) // N_RNN_BLOCKS]:
            job()
    xe_ref[0:SUBLANES, :] = xe_ref[tm:tm + SUBLANES, :]

    _rglru_scan(a_ref, u_ref, carry_ref)
    for n in range(N_RNN_BLOCKS):
        sl = slice(n * RNN_BLOCK, (n + 1) * RNN_BLOCK)
        y_ref[0, :, sl] = (gg_ref[:, sl] * u_ref[n, 0:tm, :]).astype(BF16)


def _in_projection(x, mod, g1, w_a, w_qkv, w_flt, b_f, w_mg, conv_w, conv_b, w_lru, b_la, b_li, lam):
    bsz, s, d = x.shape
    d_rnn = w_a.shape[1] // 2
    tm = min(IN_TILE, s)
    tok = lambda width: pl.BlockSpec((1, tm, width), lambda b, i: (b, i, 0))
    head = pl.BlockSpec((1, N_HEADS, tm, HEAD_DIM), lambda b, i: (b, 0, i, 0))
    row = _const_spec((1, d_rnn))
    return pl.pallas_call(
        _in_kernel,
        grid=(bsz, s // tm),
        in_specs=[tok(d),
                  pl.BlockSpec((1, 1, N_MOD * d), lambda b, i: (b, 0, 0)),
                  _const_spec((1, d)),
                  _const_spec(w_a.shape), _const_spec(w_qkv.shape), _const_spec(w_flt.shape),
                  _const_spec((N_HEADS, 1)), _const_spec(w_mg.shape),
                  _const_spec(conv_w.shape), row, _const_spec(w_lru.shape), row, row, row],
        out_specs=[tok(d_rnn), head, head, head,
                   pl.BlockSpec((1, N_HEADS, tm), lambda b, i: (b, 0, i)),
                   tok(2 * d)],
        out_shape=[jax.ShapeDtypeStruct((bsz, s, d_rnn), BF16),
                   jax.ShapeDtypeStruct((bsz, N_HEADS, s, HEAD_DIM), BF16),
                   jax.ShapeDtypeStruct((bsz, N_HEADS, s, HEAD_DIM), BF16),
                   jax.ShapeDtypeStruct((bsz, N_HEADS, s, HEAD_DIM), BF16),
                   jax.ShapeDtypeStruct((bsz, N_HEADS, s), F32),
                   jax.ShapeDtypeStruct((bsz, s, 2 * d), BF16)],
        scratch_shapes=[pltpu.VMEM((tm + SUBLANES, d_rnn), F32), pltpu.VMEM((tm, d_rnn), F32)]
        + [pltpu.VMEM((N_RNN_BLOCKS, tm + SUBLANES, RNN_BLOCK), F32)] * 2
        + [pltpu.VMEM((SUBLANES, d_rnn), F32)],
        compiler_params=_params(2),
        name="in_projection",
    )(x, mod.reshape(bsz, 1, N_MOD * d), g1.reshape(1, d), w_a, w_qkv, w_flt,
      b_f.reshape(N_HEADS, 1), w_mg, conv_w, conv_b.reshape(1, d_rnn), w_lru, b_la.reshape(1, d_rnn),
      b_li.reshape(1, d_rnn), lam.reshape(1, d_rnn))


def _fcum_kernel(lf_ref, o_ref):
    rows, s = lf_ref.shape
    cw = min(FCUM_CHUNK, s)
    r_i = lax.broadcasted_iota(jnp.int32, (cw, cw), 0)
    c_i = lax.broadcasted_iota(jnp.int32, (cw, cw), 1)
    tri = (r_i <= c_i).astype(BF16)
    carry = jnp.zeros((rows, 1), F32)
    for c0 in range(0, s, cw):
        v = lf_ref[:, c0:c0 + cw]
        hi = v.astype(BF16)
        r1 = v - hi.astype(F32)
        mid = r1.astype(BF16)
        lo = (r1 - mid.astype(F32)).astype(BF16)
        acc = jnp.dot(hi, tri, preferred_element_type=F32)
        acc = acc + jnp.dot(mid, tri, preferred_element_type=F32)
        acc = acc + jnp.dot(lo, tri, preferred_element_type=F32)
        out = acc + carry
        o_ref[:, c0:c0 + cw] = out * LOG2E
        carry = out[:, cw - 1:cw]


def _forget_cumsum(lf):
    rows, s = lf.shape
    return pl.pallas_call(
        _fcum_kernel,
        out_shape=jax.ShapeDtypeStruct((rows, s), F32),
        compiler_params=pltpu.CompilerParams(vmem_limit_bytes=VMEM_LIMIT_BYTES),
        name="forget_cumsum",
    )(lf)


def _rglru_gates(n, xe_ref, cw_ref, cb_ref, w_ref, ba_ref, bi_ref, sp_scaled, a_ref, u_ref):
    ts = xe_ref.shape[0] - SUBLANES
    kw = cw_ref.shape[0]
    sl = slice(n * RNN_BLOCK, (n + 1) * RNN_BLOCK)
    xc = cb_ref[:, sl]
    for k in range(kw):
        xc = xc + xe_ref[pl.ds(SUBLANES - (kw - 1) + k, ts), sl] * cw_ref[k:k + 1, sl]
    gates = jnp.dot(xc.astype(BF16), w_ref[n], preferred_element_type=F32)
    r = jax.nn.sigmoid(gates[:, 0:RNN_BLOCK] + ba_ref[:, sl])
    i = jax.nn.sigmoid(gates[:, RNN_BLOCK:2 * RNN_BLOCK] + bi_ref[:, sl])
    log_a = r * sp_scaled[:, sl]
    t = jnp.tanh(log_a)
    w = -2.0 * t
    mult = jnp.where(w > 0.0, w * lax.rsqrt(w * (1.0 - t)), 0.0)
    a_ref[n, 0:ts, :] = jnp.exp(log_a)
    u_ref[n, 0:ts, :] = mult * (i * xc)
    a_ref[n, ts:ts + SUBLANES, :] = jnp.ones((SUBLANES, RNN_BLOCK), F32)
    u_ref[n, ts:ts + SUBLANES, :] = jnp.zeros((SUBLANES, RNN_BLOCK), F32)


def _rglru_scan(a_ref, u_ref, carry_ref):
    pitch = (a_ref.shape[1] - SUBLANES) // SUBLANES + 1
    n_unroll = 5 if pitch % 5 == 0 else 1

    def summarize(t, carry):
        idx = pl.ds(t, SUBLANES, stride=pitch)
        out = []
        for n, (h, p) in enumerate(carry):
            a = a_ref[n, idx, :]
            out.append((a * h + u_ref[n, idx, :], a * p))
        return tuple(out)

    ends = lax.fori_loop(
        0, pitch, summarize,
        tuple((jnp.zeros((SUBLANES, RNN_BLOCK), F32), jnp.ones((SUBLANES, RNN_BLOCK), F32))
              for _ in range(N_RNN_BLOCKS)), unroll=n_unroll)

    sub = lax.broadcasted_iota(jnp.int32, (SUBLANES, RNN_BLOCK), 0)
    starts = []
    for n, (h_end, p_end) in enumerate(ends):
        sl = slice(n * RNN_BLOCK, (n + 1) * RNN_BLOCK)
        c = carry_ref[0:1, sl]
        start = jnp.broadcast_to(c, (SUBLANES, RNN_BLOCK))
        for j in range(1, SUBLANES):
            c = h_end[j - 1:j] + p_end[j - 1:j] * c
            start = jnp.where(sub == j, c, start)
        carry_ref[0:1, sl] = h_end[SUBLANES - 1:SUBLANES] + p_end[SUBLANES - 1:SUBLANES] * c
        starts.append(start)

    def rescan(t, hs):
        idx = pl.ds(t, SUBLANES, stride=pitch)
        out = []
        for n, h in enumerate(hs):
            h = a_ref[n, idx, :] * h + u_ref[n, idx, :]
            u_ref[n, idx, :] = h
            out.append(h)
        return tuple(out)

    lax.fori_loop(0, pitch, rescan, tuple(starts), unroll=n_unroll)


def _attn_kernel(q_ref, k_ref, v_ref, f_ref, o_ref):
    hb, s_len = q_ref.shape[1], q_ref.shape[2]
    tq = min(ATTN_TILE, s_len)
    nt = (((1,), (1,)), ((), ()))
    causal = (lax.broadcasted_iota(jnp.int32, (tq, tq), 0)
              >= lax.broadcasted_iota(jnp.int32, (tq, tq), 1))

    def scores(hh, i):
        lo, hi = i * tq, (i + 1) * tq
        q = q_ref[0, hh, lo:hi, :]
        f_q = f_ref[0, 0, hh:hh + 1, lo:hi]
        f_t = jnp.transpose(jnp.broadcast_to(f_q, (HEAD_DIM, tq)))[:, 0:1]
        s_d = lax.dot_general(q, k_ref[0, hh, lo:hi, :], nt, preferred_element_type=F32) - f_q
        s_d = jnp.where(causal, s_d, -jnp.inf)
        m = jnp.max(s_d, axis=-1, keepdims=True)
        s_o = None
        if i > 0:
            s_o = (lax.dot_general(q, k_ref[0, hh, 0:lo, :], nt, preferred_element_type=F32)
                   - f_ref[0, 0, hh:hh + 1, 0:lo])
            m = jnp.maximum(m, jnp.max(s_o, axis=-1, keepdims=True))
        return s_d, s_o, (m + f_t) - f_t

    def output(hh, i, s_d, s_o, shift):
        lo, hi = i * tq, (i + 1) * tq
        p_d = jnp.exp2(s_d - shift)
        l = jnp.sum(p_d, axis=-1, keepdims=True)
        acc = jnp.dot(p_d.astype(BF16), v_ref[0, hh, lo:hi, :], preferred_element_type=F32)
        if i > 0:
            p_o = jnp.exp2(s_o - shift)
            l = l + jnp.sum(p_o, axis=-1, keepdims=True)
            acc = acc + jnp.dot(p_o.astype(BF16), v_ref[0, hh, 0:lo, :], preferred_element_type=F32)
        o_ref[0, lo:hi, hh * HEAD_DIM:(hh + 1) * HEAD_DIM] = (acc * (1.0 / l)).astype(BF16)

    n_q = s_len // tq
    staged = [scores(hh, 0) for hh in range(hb)]
    for i in range(n_q):
        nxt = [scores(hh, i + 1) for hh in range(hb)] if i + 1 < n_q else None
        for hh in range(hb):
            output(hh, i, *staged[hh])
        staged = nxt


def _attention(q, k, v, f_cum):
    bsz, n_heads, s, dh = q.shape
    hb = ATTN_HEADS_PER_STEP
    f4 = f_cum.reshape(bsz, n_heads // hb, hb, s)
    qkv = pl.BlockSpec((1, hb, s, dh), lambda b, g: (b, g, 0, 0))
    return pl.pallas_call(
        _attn_kernel,
        grid=(bsz, n_heads // hb),
        in_specs=[qkv, qkv, qkv, pl.BlockSpec((1, 1, hb, s), lambda b, g: (b, g, 0, 0))],
        out_specs=pl.BlockSpec((1, s, hb * dh), lambda b, g: (b, 0, g)),
        out_shape=jax.ShapeDtypeStruct((bsz, s, n_heads * dh), BF16),
        compiler_params=_params(2),
        name="forgetting_attention",
    )(q, k, v, f4)


def _back_kernel(x_ref, yr_ref, ya_ref, g_ref, mod_ref, g2_ref, wpr_ref, wpa_ref, wo_ref,
                 wup_ref, cw_ref, cb_ref, wdn_ref, modf_ref, gf_ref, o_ref, ge_ref, halo_ref, *, final):
    tm, d = x_ref.shape[1], x_ref.shape[2]
    d_ff = wdn_ref.shape[0]
    kw = cw_ref.shape[0]
    mod = lambda idx: mod_ref[0, :, idx * d:(idx + 1) * d]

    @pl.when(pl.program_id(1) == 0)
    def _():
        halo_ref[...] = jnp.zeros_like(halo_ref)

    pr = jnp.dot(yr_ref[0], wpr_ref[...], preferred_element_type=F32)
    pa = jnp.dot(ya_ref[0], wpa_ref[...], preferred_element_type=F32)
    merged = g_ref[0, :, 0:d].astype(F32) * pr + g_ref[0, :, d:2 * d].astype(F32) * pa
    x1 = x_ref[0] + mod(2) * jnp.dot(merged.astype(BF16), wo_ref[...], preferred_element_type=F32)

    h2 = _rmsnorm_mod(x1, g2_ref[...], mod(3), mod(4)).astype(BF16)
    chunks = _chunks(d_ff, FFN_CHUNK)

    def up(c):
        c0, cw = chunks[c]
        return (jnp.dot(h2, wup_ref[:, c0:c0 + cw], preferred_element_type=F32),
                jnp.dot(h2, wup_ref[:, d_ff + c0:d_ff + c0 + cw], preferred_element_type=F32))

    def activate(c, gfc, ufc):
        c0, cw = chunks[c]
        cs = slice(c0, c0 + cw)
        ge = ge_ref.at[c % 2]
        ge[0:SUBLANES, 0:cw] = halo_ref[:, cs]
        ge[SUBLANES:SUBLANES + tm, 0:cw] = gfc
        conv = cb_ref[:, cs]
        for k in range(kw - 1):
            conv = conv + ge[pl.ds(SUBLANES - (kw - 1) + k, tm), 0:cw] * cw_ref[k:k + 1, cs]
        conv = conv + gfc * cw_ref[kw - 1:kw, cs]
        halo_ref[:, cs] = ge[tm:tm + SUBLANES, 0:cw]
        return (_gelu_tanh(conv) * ufc).astype(BF16)

    def down(c, act):
        c0, cw = chunks[c]
        return jnp.dot(act, wdn_ref[c0:c0 + cw, :], preferred_element_type=F32)

    n = len(chunks)
    y = None
    nxt = up(0)
    act_prev = None
    for c in range(n):
        cur, nxt = nxt, (up(c + 1) if c + 1 < n else None)
        act = activate(c, *cur)
        if act_prev is not None:
            part = down(c - 1, act_prev)
            y = part if y is None else y + part
        act_prev = act
    part = down(n - 1, act_prev)
    y = part if y is None else y + part

    x2 = x1 + mod(5) * y
    if final:
        x2 = _rmsnorm_mod(x2, gf_ref[...], modf_ref[0, :, 0:d], modf_ref[0, :, d:2 * d])
    o_ref[0] = x2


def _back_half(x, y_rnn, y_attn, g, mod, g2, w_pr, w_pa, w_o, w_up, conv_w, conv_b, w_dn,
               mod_f, g_f, final):
    bsz, s, d = x.shape
    d_ff = w_dn.shape[0]
    tm = min(BACK_TILE, s)
    tok = lambda width: pl.BlockSpec((1, tm, width), lambda b, i: (b, i, 0))
    return pl.pallas_call(
        functools.partial(_back_kernel, final=final),
        grid=(bsz, s // tm),
        in_specs=[tok(d), tok(y_rnn.shape[2]), tok(y_attn.shape[2]), tok(2 * d),
                  pl.BlockSpec((1, 1, N_MOD * d), lambda b, i: (b, 0, 0)),
                  _const_spec((1, d)),
                  _const_spec(w_pr.shape), _const_spec(w_pa.shape), _const_spec(w_o.shape),
                  _const_spec(w_up.shape), _const_spec(conv_w.shape), _const_spec((1, d_ff)),
                  _const_spec(w_dn.shape),
                  pl.BlockSpec((1, 1, 2 * d), lambda b, i: (b, 0, 0)),
                  _const_spec((1, d))],
        out_specs=tok(d),
        out_shape=jax.ShapeDtypeStruct((bsz, s, d), F32),
        scratch_shapes=[pltpu.VMEM((2, tm + SUBLANES, FFN_CHUNK), F32),
                        pltpu.VMEM((SUBLANES, d_ff), F32)],
        compiler_params=_params(2),
        name="merge_ffn",
    )(x, y_rnn, y_attn, g, mod.reshape(bsz, 1, N_MOD * d), g2.reshape(1, d), w_pr, w_pa, w_o,
      w_up, conv_w, conv_b.reshape(1, d_ff), w_dn, mod_f.reshape(bsz, 1, 2 * d), g_f.reshape(1, d))


def kernel(x, c, w_ada, b_ada, g_norm1, w_in, w_rnn_conv, b_rnn_conv, w_lru_a, b_lru_a, w_lru_i, b_lru_i, lru_lambda, b_fgate, w_proj_rnn, w_proj_attn, w_out, g_norm2, w_ffn_up, w_ffn_conv, b_ffn_conv, w_ffn_down, w_ada_final, b_ada_final, g_final):
    bsz, s, d = x.shape
    depth = w_ada.shape[0]
    d_rnn = w_rnn_conv.shape[2]
    d_attn = N_HEADS * HEAD_DIM
    mod_f = _modulation(c, w_ada_final, b_ada_final)
    for l in range(depth):
        mod = _modulation(c, w_ada[l], b_ada[l])
        o_qkv = 2 * d_rnn
        o_fl = o_qkv + 3 * d_attn
        o_mg = o_fl + N_HEADS
        w_a = w_in[l][:, :o_qkv].astype(BF16)
        w_qkv = w_in[l][:, o_qkv:o_fl].astype(BF16)
        w_flt = jnp.pad(w_in[l][:, o_fl:o_mg].T, ((0, 2 * SUBLANES - N_HEADS), (0, 0))).astype(BF16)
        w_mg = w_in[l][:, o_mg:].astype(BF16)

        y_rnn, q, k, v, lf, g = _in_projection(
            x, mod, g_norm1[l], w_a, w_qkv, w_flt, b_fgate[l], w_mg, w_rnn_conv[l], b_rnn_conv[l],
            jnp.concatenate([w_lru_a[l], w_lru_i[l]], axis=-1).astype(BF16), b_lru_a[l], b_lru_i[l],
            lru_lambda[l])
        f_cum = _forget_cumsum(lf.reshape(bsz * N_HEADS, s)).reshape(bsz, N_HEADS, s)
        y_attn = _attention(q, k, v, f_cum)
        x = _back_half(x, y_rnn, y_attn, g, mod, g_norm2[l], w_proj_rnn[l].astype(BF16),
                       w_proj_attn[l].astype(BF16), w_out[l].astype(BF16), w_ffn_up[l].astype(BF16),
                       w_ffn_conv[l], b_ffn_conv[l], w_ffn_down[l].astype(BF16), mod_f, g_final,
                       final=(l == depth - 1))
    return x
```

```python
import functools
import math

import jax
import jax.numpy as jnp
from jax import lax
from jax.experimental import pallas as pl
from jax.experimental.pallas import tpu as pltpu

F32 = jnp.float32
BF16 = jnp.bfloat16

N_RNN_BLOCKS = 10
RNN_BLOCK = 128
LRU_C = 8.0
N_HEADS = 8
HEAD_DIM = 128
RMS_EPS = 1e-6
N_MOD = 6
LOG2E = math.log2(math.e)

V7X_VMEM_BYTES = 64 * 1024 * 1024
VMEM_LIMIT_BYTES = V7X_VMEM_BYTES - 6 * 1024 * 1024
SUBLANES = 8

IN_TILE = 512
ATTN_TILE = 256
ATTN_HEADS_PER_STEP = 4
BACK_TILE = 512
FFN_CHUNK = 1024
MM_CHUNK = 512
MOD_TILE = 1024
FCUM_CHUNK = 256


def _const_spec(shape):
    nd = len(shape)
    return pl.BlockSpec(shape, lambda *_: (0,) * nd, pipeline_mode=pl.Buffered(1))


def _params(n_grid):
    return pltpu.CompilerParams(
        dimension_semantics=("arbitrary",) * n_grid, vmem_limit_bytes=VMEM_LIMIT_BYTES)


def _rmsnorm_mod(x, g, shift, scale):
    ms = jnp.mean(x * x, axis=-1, keepdims=True)
    y = x * lax.rsqrt(ms + RMS_EPS)
    return y * (g * (1.0 + scale)) + shift


def _gelu_tanh(x):
    c = math.sqrt(2.0 / math.pi)
    return x * (0.5 + 0.5 * jnp.tanh(x * (c + (0.044715 * c) * (x * x))))


def _softplus(x):
    return jnp.maximum(x, 0.0) + jnp.log1p(jnp.exp(-jnp.abs(x)))


def _chunks(n, width):
    return [(c0, min(width, n - c0)) for c0 in range(0, n, width)]


def _mod_kernel(c_ref, w_ref, b_ref, o_ref):
    c = c_ref[...]
    c_act = (c * jax.nn.sigmoid(c)).astype(BF16)
    o_ref[...] = jnp.dot(c_act, w_ref[...].astype(BF16), preferred_element_type=F32) + b_ref[...]


def _modulation(c, w, b):
    bsz, d = c.shape
    n = w.shape[1]
    tn = min(MOD_TILE, n)
    return pl.pallas_call(
        _mod_kernel,
        grid=(n // tn,),
        in_specs=[pl.BlockSpec((bsz, d), lambda j: (0, 0)),
                  pl.BlockSpec((d, tn), lambda j: (0, j)),
                  pl.BlockSpec((1, tn), lambda j: (0, j))],
        out_specs=pl.BlockSpec((bsz, tn), lambda j: (0, j)),
        out_shape=jax.ShapeDtypeStruct((bsz, n), F32),
        compiler_params=_params(1),
        name="adaln_mod",
    )(c, w, b.reshape(1, n))


def _in_kernel(x_ref, mod_ref, g1_ref, wa_ref, wqkv_ref, wflt_ref, bf_ref, wmg_ref,
               cw_ref, cb_ref, wlru_ref, bla_ref, bli_ref, lam_ref,
               y_ref, q_ref, k_ref, v_ref, lf_ref, g_ref,
               xe_ref, gg_ref, a_ref, u_ref, carry_ref):
    tm, d = x_ref.shape[1], x_ref.shape[2]
    d_rnn = y_ref.shape[2]
    d_attn = N_HEADS * HEAD_DIM

    @pl.when(pl.program_id(1) == 0)
    def _():
        xe_ref[0:SUBLANES, :] = jnp.zeros((SUBLANES, d_rnn), F32)
        carry_ref[...] = jnp.zeros_like(carry_ref)

    h = _rmsnorm_mod(x_ref[0], g1_ref[...], mod_ref[0, :, 0:d], mod_ref[0, :, d:2 * d]).astype(BF16)

    def head_chunk(out_ref, base, scale, c0, cw):
        z = jnp.dot(h, wqkv_ref[:, base + c0:base + c0 + cw], preferred_element_type=F32)
        if scale is not None:
            z = z * scale
        for hh in range(cw // HEAD_DIM):
            out_ref[0, c0 // HEAD_DIM + hh] = z[:, hh * HEAD_DIM:(hh + 1) * HEAD_DIM].astype(BF16)

    def forget_logits():
        fl_t = lax.dot_general(wflt_ref[...], h, (((1,), (1,)), ((), ())), preferred_element_type=F32)
        fl_t = fl_t[0:N_HEADS] + bf_ref[...]
        lf_ref[0] = jnp.minimum(fl_t, 0.0) - jnp.log1p(jnp.exp(-jnp.abs(fl_t)))

    def merge_gate_chunk(c0, cw):
        z = jnp.dot(h, wmg_ref[:, c0:c0 + cw], preferred_element_type=F32)
        g_ref[0, :, c0:c0 + cw] = jax.nn.sigmoid(z).astype(BF16)

    pending = [functools.partial(head_chunk, out_ref, base, scale, c0, cw)
               for out_ref, base, scale in ((q_ref, 0, HEAD_DIM ** -0.5 * LOG2E), (k_ref, d_attn, None),
                                            (v_ref, 2 * d_attn, None))
               for c0, cw in _chunks(d_attn, MM_CHUNK)]
    pending += [functools.partial(merge_gate_chunk, c0, cw) for c0, cw in _chunks(2 * d, MM_CHUNK)]
    pending.append(forget_logits)

    def gelu_gate_chunk(c0, cw):
        gg_ref[:, c0:c0 + cw] = _gelu_tanh(jnp.dot(
            h, wa_ref[:, d_rnn + c0:d_rnn + c0 + cw], preferred_element_type=F32))

    pending += [functools.partial(gelu_gate_chunk, c0, cw) for c0, cw in _chunks(d_rnn, MM_CHUNK)]

    for c0, cw in _chunks(d_rnn, MM_CHUNK):
        xe_ref[SUBLANES:SUBLANES + tm, c0:c0 + cw] = jnp.dot(
            h, wa_ref[:, c0:c0 + cw], preferred_element_type=F32)
    sp_scaled = -LRU_C * _softplus(-lam_ref[...])
    for n in range(N_RNN_BLOCKS):
        _rglru_gates(n, xe_ref, cw_ref, cb_ref, wlru_ref, bla_ref, bli_ref, sp_scaled, a_ref, u_ref)
        for job in pending[n * len(pending) // N_RNN_BLOCKS:(n + 1) * len(pending) // N_RNN_BLOCKS]:
            job()
    xe_ref[0:SUBLANES, :] = xe_ref[tm:tm + SUBLANES, :]

    _rglru_scan(a_ref, u_ref, carry_ref)
    for n in range(N_RNN_BLOCKS):
        sl = slice(n * RNN_BLOCK, (n + 1) * RNN_BLOCK)
        y_ref[0, :, sl] = (gg_ref[:, sl] * u_ref[n, 0:tm, :]).astype(BF16)


def _in_projection(x, mod, g1, w_a, w_qkv, w_flt, b_f, w_mg, conv_w, conv_b, w_lru, b_la, b_li, lam):
    bsz, s, d = x.shape
    d_rnn = w_a.shape[1] // 2
    tm = min(IN_TILE, s)
    tok = lambda width: pl.BlockSpec((1, tm, width), lambda b, i: (b, i, 0))
    head = pl.BlockSpec((1, N_HEADS, tm, HEAD_DIM), lambda b, i: (b, 0, i, 0))
    row = _const_spec((1, d_rnn))
    return pl.pallas_call(
        _in_kernel,
        grid=(bsz, s // tm),
        in_specs=[tok(d),
                  pl.BlockSpec((1, 1, N_MOD * d), lambda b, i: (b, 0, 0)),
                  _const_spec((1, d)),
                  _const_spec(w_a.shape), _const_spec(w_qkv.shape), _const_spec(w_flt.shape),
                  _const_spec((N_HEADS, 1)), _const_spec(w_mg.shape),
                  _const_spec(conv_w.shape), row, _const_spec(w_lru.shape), row, row, row],
        out_specs=[tok(d_rnn), head, head, head,
                   pl.BlockSpec((1, N_HEADS, tm), lambda b, i: (b, 0, i)),
                   tok(2 * d)],
        out_shape=[jax.ShapeDtypeStruct((bsz, s, d_rnn), BF16),
                   jax.ShapeDtypeStruct((bsz, N_HEADS, s, HEAD_DIM), BF16),
                   jax.ShapeDtypeStruct((bsz, N_HEADS, s, HEAD_DIM), BF16),
                   jax.ShapeDtypeStruct((bsz, N_HEADS, s, HEAD_DIM), BF16),
                   jax.ShapeDtypeStruct((bsz, N_HEADS, s), F32),
                   jax.ShapeDtypeStruct((bsz, s, 2 * d), BF16)],
        scratch_shapes=[pltpu.VMEM((tm + SUBLANES, d_rnn), F32), pltpu.VMEM((tm, d_rnn), F32)]
        + [pltpu.VMEM((N_RNN_BLOCKS, tm + SUBLANES, RNN_BLOCK), F32)] * 2
        + [pltpu.VMEM((SUBLANES, d_rnn), F32)],
        compiler_params=_params(2),
        name="in_projection",
    )(x, mod.reshape(bsz, 1, N_MOD * d), g1.reshape(1, d), w_a, w_qkv, w_flt,
      b_f.reshape(N_HEADS, 1), w_mg, conv_w, conv_b.reshape(1, d_rnn), w_lru, b_la.reshape(1, d_rnn),
      b_li.reshape(1, d_rnn), lam.reshape(1, d_rnn))


def _fcum_kernel(lf_ref, o_ref):
    rows, s = lf_ref.shape
    cw = min(FCUM_CHUNK, s)
    r_i = lax.broadcasted_iota(jnp.int32, (cw, cw), 0)
    c_i = lax.broadcasted_iota(jnp.int32, (cw, cw), 1)
    tri = (r_i <= c_i).astype(BF16)
    carry = jnp.zeros((rows, 1), F32)
    for c0 in range(0, s, cw):
        v = lf_ref[:, c0:c0 + cw]
        hi = v.astype(BF16)
        r1 = v - hi.astype(F32)
        mid = r1.astype(BF16)
        lo = (r1 - mid.astype(F32)).astype(BF16)
        acc = jnp.dot(hi, tri, preferred_element_type=F32)
        acc = acc + jnp.dot(mid, tri, preferred_element_type=F32)
        acc = acc + jnp.dot(lo, tri, preferred_element_type=F32)
        out = acc + carry
        o_ref[:, c0:c0 + cw] = out * LOG2E
        carry = out[:, cw - 1:cw]


def _forget_cumsum(lf):
    rows, s = lf.shape
    return pl.pallas_call(
        _fcum_kernel,
        out_shape=jax.ShapeDtypeStruct((rows, s), F32),
        compiler_params=pltpu.CompilerParams(vmem_limit_bytes=VMEM_LIMIT_BYTES),
        name="forget_cumsum",
    )(lf)


def _rglru_gates(n, xe_ref, cw_ref, cb_ref, w_ref, ba_ref, bi_ref, sp_scaled, a_ref, u_ref):
    ts = xe_ref.shape[0] - SUBLANES
    kw = cw_ref.shape[0]
    sl = slice(n * RNN_BLOCK, (n + 1) * RNN_BLOCK)
    ext = xe_ref[:, sl]
    xc = cb_ref[:, sl]
    for k in range(kw - 1):
        delayed = pltpu.roll(ext, kw - 1 - k, axis=0)[SUBLANES:SUBLANES + ts]
        xc = xc + delayed * cw_ref[k:k + 1, sl]
    xc = xc + ext[SUBLANES:SUBLANES + ts] * cw_ref[kw - 1:kw, sl]
    gates = jnp.dot(xc.astype(BF16), w_ref[n], preferred_element_type=F32)
    r = jax.nn.sigmoid(gates[:, 0:RNN_BLOCK] + ba_ref[:, sl])
    i = jax.nn.sigmoid(gates[:, RNN_BLOCK:2 * RNN_BLOCK] + bi_ref[:, sl])
    log_a = r * sp_scaled[:, sl]
    t = jnp.tanh(log_a)
    w = -2.0 * t
    mult = jnp.where(w > 0.0, w * lax.rsqrt(w * (1.0 - t)), 0.0)
    a_ref[n, 0:ts, :] = jnp.exp(log_a)
    u_ref[n, 0:ts, :] = mult * (i * xc)
    a_ref[n, ts:ts + SUBLANES, :] = jnp.ones((SUBLANES, RNN_BLOCK), F32)
    u_ref[n, ts:ts + SUBLANES, :] = jnp.zeros((SUBLANES, RNN_BLOCK), F32)


def _rglru_scan(a_ref, u_ref, carry_ref):
    pitch = (a_ref.shape[1] - SUBLANES) // SUBLANES + 1
    n_unroll = 5 if pitch % 5 == 0 else 1

    def summarize(t, carry):
        idx = pl.ds(t, SUBLANES, stride=pitch)
        out = []
        for n, (h, p) in enumerate(carry):
            a = a_ref[n, idx, :]
            out.append((a * h + u_ref[n, idx, :], a * p))
        return tuple(out)

    ends = lax.fori_loop(
        0, pitch, summarize,
        tuple((jnp.zeros((SUBLANES, RNN_BLOCK), F32), jnp.ones((SUBLANES, RNN_BLOCK), F32))
              for _ in range(N_RNN_BLOCKS)), unroll=n_unroll)

    sub = lax.broadcasted_iota(jnp.int32, (SUBLANES, RNN_BLOCK), 0)
    starts = []
    for n, (h_end, p_end) in enumerate(ends):
        sl = slice(n * RNN_BLOCK, (n + 1) * RNN_BLOCK)
        c = carry_ref[0:1, sl]
        start = jnp.broadcast_to(c, (SUBLANES, RNN_BLOCK))
        for j in range(1, SUBLANES):
            c = h_end[j - 1:j] + p_end[j - 1:j] * c
            start = jnp.where(sub == j, c, start)
        carry_ref[0:1, sl] = h_end[SUBLANES - 1:SUBLANES] + p_end[SUBLANES - 1:SUBLANES] * c
        starts.append(start)

    def rescan(t, hs):
        idx = pl.ds(t, SUBLANES, stride=pitch)
        out = []
        for n, h in enumerate(hs):
            h = a_ref[n, idx, :] * h + u_ref[n, idx, :]
            u_ref[n, idx, :] = h
            out.append(h)
        return tuple(out)

    lax.fori_loop(0, pitch, rescan, tuple(starts), unroll=n_unroll)


def _attn_kernel(q_ref, k_ref, v_ref, f_ref, o_ref):
    hb, s_len = q_ref.shape[1], q_ref.shape[2]
    tq = min(ATTN_TILE, s_len)
    nt = (((1,), (1,)), ((), ()))
    causal = (lax.broadcasted_iota(jnp.int32, (tq, tq), 0)
              >= lax.broadcasted_iota(jnp.int32, (tq, tq), 1))

    def scores(hh, i):
        lo, hi = i * tq, (i + 1) * tq
        q = q_ref[0, hh, lo:hi, :]
        f_q = f_ref[0, 0, hh:hh + 1, lo:hi]
        f_t = jnp.transpose(jnp.broadcast_to(f_q, (HEAD_DIM, tq)))[:, 0:1]
        s_d = lax.dot_general(q, k_ref[0, hh, lo:hi, :], nt, preferred_element_type=F32) - f_q
        s_d = jnp.where(causal, s_d, -jnp.inf)
        m = jnp.max(s_d, axis=-1, keepdims=True)
        s_o = None
        if i > 0:
            s_o = (lax.dot_general(q, k_ref[0, hh, 0:lo, :], nt, preferred_element_type=F32)
                   - f_ref[0, 0, hh:hh + 1, 0:lo])
            m = jnp.maximum(m, jnp.max(s_o, axis=-1, keepdims=True))
        return s_d, s_o, (m + f_t) - f_t

    ones = jnp.ones((s_len, HEAD_DIM), BF16)
    v_aug = [jnp.concatenate([v_ref[0, hh], ones], axis=1) for hh in range(hb)]

    def output(hh, i, s_d, s_o, shift):
        lo, hi = i * tq, (i + 1) * tq
        p_d = jnp.exp2(s_d - shift)
        acc = jnp.dot(p_d.astype(BF16), v_aug[hh][lo:hi], preferred_element_type=F32)
        if i > 0:
            p_o = jnp.exp2(s_o - shift)
            acc = acc + jnp.dot(p_o.astype(BF16), v_aug[hh][0:lo], preferred_element_type=F32)
        l = acc[:, HEAD_DIM:HEAD_DIM + 1]
        o_ref[0, lo:hi, hh * HEAD_DIM:(hh + 1) * HEAD_DIM] = (acc[:, 0:HEAD_DIM] * (1.0 / l)).astype(BF16)

    n_q = s_len // tq
    staged = [scores(hh, 0) for hh in range(hb)]
    for i in range(n_q):
        nxt = [scores(hh, i + 1) for hh in range(hb)] if i + 1 < n_q else None
        for hh in range(hb):
            output(hh, i, *staged[hh])
        staged = nxt


def _attention(q, k, v, f_cum):
    bsz, n_heads, s, dh = q.shape
    hb = ATTN_HEADS_PER_STEP
    f4 = f_cum.reshape(bsz, n_heads // hb, hb, s)
    qkv = pl.BlockSpec((1, hb, s, dh), lambda b, g: (b, g, 0, 0))
    return pl.pallas_call(
        _attn_kernel,
        grid=(bsz, n_heads // hb),
        in_specs=[qkv, qkv, qkv, pl.BlockSpec((1, 1, hb, s), lambda b, g: (b, g, 0, 0))],
        out_specs=pl.BlockSpec((1, s, hb * dh), lambda b, g: (b, 0, g)),
        out_shape=jax.ShapeDtypeStruct((bsz, s, n_heads * dh), BF16),
        compiler_params=_params(2),
        name="forgetting_attention",
    )(q, k, v, f4)


def _back_kernel(x_ref, yr_ref, ya_ref, g_ref, mod_ref, g2_ref, wpr_ref, wpa_ref, wo_ref,
                 wup_ref, cw_ref, cb_ref, wdn_ref, modf_ref, gf_ref, o_ref, ge_ref, halo_ref, *, final):
    tm, d = x_ref.shape[1], x_ref.shape[2]
    d_ff = wdn_ref.shape[0]
    kw = cw_ref.shape[0]
    mod = lambda idx: mod_ref[0, :, idx * d:(idx + 1) * d]

    @pl.when(pl.program_id(1) == 0)
    def _():
        halo_ref[...] = jnp.zeros_like(halo_ref)

    pr = jnp.dot(yr_ref[0], wpr_ref[...], preferred_element_type=F32)
    pa = jnp.dot(ya_ref[0], wpa_ref[...], preferred_element_type=F32)
    merged = g_ref[0, :, 0:d].astype(F32) * pr + g_ref[0, :, d:2 * d].astype(F32) * pa
    x1 = x_ref[0] + mod(2) * jnp.dot(merged.astype(BF16), wo_ref[...], preferred_element_type=F32)

    h2 = _rmsnorm_mod(x1, g2_ref[...], mod(3), mod(4)).astype(BF16)
    chunks = _chunks(d_ff, FFN_CHUNK)

    def up(c):
        c0, cw = chunks[c]
        return (jnp.dot(h2, wup_ref[:, c0:c0 + cw], preferred_element_type=F32),
                jnp.dot(h2, wup_ref[:, d_ff + c0:d_ff + c0 + cw], preferred_element_type=F32))

    def activate(c, gfc, ufc):
        c0, cw = chunks[c]
        cs = slice(c0, c0 + cw)
        ge = ge_ref.at[c % 2]
        ge[0:SUBLANES, 0:cw] = halo_ref[:, cs]
        ge[SUBLANES:SUBLANES + tm, 0:cw] = gfc
        ext = ge[0:SUBLANES + tm, 0:cw]
        conv = cb_ref[:, cs]
        for k in range(kw - 1):
            delayed = pltpu.roll(ext, kw - 1 - k, axis=0)[SUBLANES:SUBLANES + tm]
            conv = conv + delayed * cw_ref[k:k + 1, cs]
        conv = conv + gfc * cw_ref[kw - 1:kw, cs]
        halo_ref[:, cs] = ge[tm:tm + SUBLANES, 0:cw]
        return (_gelu_tanh(conv) * ufc).astype(BF16)

    def down(c, act):
        c0, cw = chunks[c]
        return jnp.dot(act, wdn_ref[c0:c0 + cw, :], preferred_element_type=F32)

    n = len(chunks)
    y = None
    nxt = up(0)
    act_prev = None
    for c in range(n):
        cur, nxt = nxt, (up(c + 1) if c + 1 < n else None)
        act = activate(c, *cur)
        if act_prev is not None:
            part = down(c - 1, act_prev)
            y = part if y is None else y + part
        act_prev = act
    part = down(n - 1, act_prev)
    y = part if y is None else y + part

    x2 = x1 + mod(5) * y
    if final:
        x2 = _rmsnorm_mod(x2, gf_ref[...], modf_ref[0, :, 0:d], modf_ref[0, :, d:2 * d])
    o_ref[0] = x2


def _back_half(x, y_rnn, y_attn, g, mod, g2, w_pr, w_pa, w_o, w_up, conv_w, conv_b, w_dn,
               mod_f, g_f, final):
    bsz, s, d = x.shape
    d_ff = w_dn.shape[0]
    tm = min(BACK_TILE, s)
    tok = lambda width: pl.BlockSpec((1, tm, width), lambda b, i: (b, i, 0))
    return pl.pallas_call(
        functools.partial(_back_kernel, final=final),
        grid=(bsz, s // tm),
        in_specs=[tok(d), tok(y_rnn.shape[2]), tok(y_attn.shape[2]), tok(2 * d),
                  pl.BlockSpec((1, 1, N_MOD * d), lambda b, i: (b, 0, 0)),
                  _const_spec((1, d)),
                  _const_spec(w_pr.shape), _const_spec(w_pa.shape), _const_spec(w_o.shape),
                  _const_spec(w_up.shape), _const_spec(conv_w.shape), _const_spec((1, d_ff)),
                  _const_spec(w_dn.shape),
                  pl.BlockSpec((1, 1, 2 * d), lambda b, i: (b, 0, 0)),
                  _const_spec((1, d))],
        out_specs=tok(d),
        out_shape=jax.ShapeDtypeStruct((bsz, s, d), F32),
        scratch_shapes=[pltpu.VMEM((2, tm + SUBLANES, FFN_CHUNK), F32),
                        pltpu.VMEM((SUBLANES, d_ff), F32)],
        compiler_params=_params(2),
        name="merge_ffn",
    )(x, y_rnn, y_attn, g, mod.reshape(bsz, 1, N_MOD * d), g2.reshape(1, d), w_pr, w_pa, w_o,
      w_up, conv_w, conv_b.reshape(1, d_ff), w_dn, mod_f.reshape(bsz, 1, 2 * d), g_f.reshape(1, d))


def kernel(x, c, w_ada, b_ada, g_norm1, w_in, w_rnn_conv, b_rnn_conv, w_lru_a, b_lru_a, w_lru_i, b_lru_i, lru_lambda, b_fgate, w_proj_rnn, w_proj_attn, w_out, g_norm2, w_ffn_up, w_ffn_conv, b_ffn_conv, w_ffn_down, w_ada_final, b_ada_final, g_final):
    bsz, s, d = x.shape
    depth = w_ada.shape[0]
    d_rnn = w_rnn_conv.shape[2]
    d_attn = N_HEADS * HEAD_DIM
    mod_f = _modulation(c, w_ada_final, b_ada_final)
    for l in range(depth):
        mod = _modulation(c, w_ada[l], b_ada[l])
        o_qkv = 2 * d_rnn
        o_fl = o_qkv + 3 * d_attn
        o_mg = o_fl + N_HEADS
        w_a = w_in[l][:, :o_qkv].astype(BF16)
        w_qkv = w_in[l][:, o_qkv:o_fl].astype(BF16)
        w_flt = jnp.pad(w_in[l][:, o_fl:o_mg].T, ((0, 2 * SUBLANES - N_HEADS), (0, 0))).astype(BF16)
        w_mg = w_in[l][:, o_mg:].astype(BF16)

        y_rnn, q, k, v, lf, g = _in_projection(
            x, mod, g_norm1[l], w_a, w_qkv, w_flt, b_fgate[l], w_mg, w_rnn_conv[l], b_rnn_conv[l],
            jnp.concatenate([w_lru_a[l], w_lru_i[l]], axis=-1).astype(BF16), b_lru_a[l], b_lru_i[l],
            lru_lambda[l])
        f_cum = _forget_cumsum(lf.reshape(bsz * N_HEADS, s)).reshape(bsz, N_HEADS, s)
        y_attn = _attention(q, k, v, f_cum)
        x = _back_half(x, y_rnn, y_attn, g, mod, g_norm2[l], w_proj_rnn[l].astype(BF16),
                       w_proj_attn[l].astype(BF16), w_out[l].astype(BF16), w_ffn_up[l].astype(BF16),
                       w_ffn_conv[l], b_ffn_conv[l], w_ffn_down[l].astype(BF16), mod_f, g_final,
                       final=(l == depth - 1))
    return x
```
